```python
import functools
import jax, jax.numpy as jnp
from jax import lax
import numpy as np

D_MODEL = 1024
BATCH = 8
SEQ = 2048
DEPTH = 1
DEC_BATCH = 32
DEC_SEQ = 4
PAST_LEN = 16384
PAGE_SIZE = 128

N_HEADS = 8
HEAD_DIM = 64
ATTN_WIDTH = N_HEADS * HEAD_DIM
IDX_HEADS = 8
IDX_DIM = 64
TOPK_MAX = 256
Q_BLOCK = 128
LRU_WIDTH = 512
LRU_BLOCKS = 8
LRU_BW = LRU_WIDTH // LRU_BLOCKS
LRU_CONV_W = 4
LRU_C = 8.0
D_FF = 3 * D_MODEL
FFN_CONV_W = 3
ROPE_THETA = 10000.0
EPS = 1e-6
IN_SPLITS = (ATTN_WIDTH, ATTN_WIDTH, ATTN_WIDTH, IDX_HEADS * IDX_DIM, IDX_DIM, IDX_HEADS,
             LRU_WIDTH, LRU_WIDTH, D_MODEL, D_MODEL)
D_IN = sum(IN_SPLITS)

kernel_name = "hybrid_dsa_rglru_convffn_step"


def split_cols(a, sizes):
    out, start = [], 0
    for s in sizes:
        out.append(a[..., start:start + s])
        start += s
    return out


def rmsnorm(x, g):
    xf = x.astype(jnp.float32)
    r = lax.rsqrt(jnp.mean(xf * xf, axis=-1, keepdims=True) + EPS)
    return (xf * r * g.astype(jnp.float32)).astype(x.dtype)


def rope(x, pos):
    half = x.shape[-1] // 2
    inv = jnp.power(ROPE_THETA, -jnp.arange(half, dtype=jnp.float32) / half)
    ang = pos.astype(jnp.float32)[:, None] * inv[None, :]
    cos = jnp.cos(ang)[None, :, None, :]
    sin = jnp.sin(ang)[None, :, None, :]
    xf = x.astype(jnp.float32)
    x1, x2 = xf[..., :half], xf[..., half:]
    return jnp.concatenate([x1 * cos - x2 * sin, x2 * cos + x1 * sin], axis=-1).astype(x.dtype)


def causal_dwconv(x, buf, w, b):
    T = x.shape[1]
    W = w.shape[0]
    xx = jnp.concatenate([buf.astype(x.dtype), x], axis=1)
    out = b
    for j in range(W):
        out = out + xx[:, j:j + T] * w[j]
    return out.astype(x.dtype), xx[:, xx.shape[1] - (W - 1):]


def rg_lru(x, h0, wa, ba, wx, bx, lam):
    B, T, _ = x.shape
    xb = x.reshape(B, T, LRU_BLOCKS, LRU_BW)
    r = jax.nn.sigmoid(jnp.einsum('btnc,ncd->btnd', xb, wa).reshape(B, T, LRU_WIDTH) + ba)
    i = jax.nn.sigmoid(jnp.einsum('btnc,ncd->btnd', xb, wx).reshape(B, T, LRU_WIDTH) + bx)
    log_a = -LRU_C * r.astype(jnp.float32) * jax.nn.softplus(-lam.astype(jnp.float32))
    a = jnp.exp(log_a)
    mult = jnp.sqrt(-jnp.expm1(2.0 * log_a))
    bt = mult * (i * x).astype(jnp.float32)
    bt = bt.at[:, 0].add(a[:, 0] * h0.astype(jnp.float32))

    def comb(l, r_):
        return (l[0] * r_[0], r_[0] * l[1] + r_[1])

    _, h = lax.associative_scan(comb, (a, bt), axis=1)
    return h.astype(x.dtype), h[:, -1].astype(x.dtype)


def index_scores(qi, wi, ki):
    s = jax.nn.relu(jnp.einsum('bqhd,bsd->bqhs', qi.astype(jnp.float32), ki.astype(jnp.float32)) * IDX_DIM ** -0.5)
    return jnp.einsum('bqhs,bqh->bqs', s, wi.astype(jnp.float32))


def sparse_attend(q, k_sel, v_sel, valid):
    s = jnp.einsum('bqhd,bqkhd->bqhk', q.astype(jnp.float32), k_sel.astype(jnp.float32)) * HEAD_DIM ** -0.5
    s = jnp.where(valid[:, :, None, :], s, -jnp.inf)
    p = jax.nn.softmax(s, axis=-1)
    return jnp.einsum('bqhk,bqkhd->bqhd', p, v_sel.astype(jnp.float32)).astype(q.dtype)


def gather_rows(a, idx):
    return jax.vmap(lambda ab, ib: ab[ib])(a, idx)


def attend_prompt(q, k, v, qi, ki, wi):
    B, T = q.shape[:2]
    topk = min(TOPK_MAX, T // 4)
    nblk = T // Q_BLOCK
    key_pos = jnp.arange(T)

    def to_blocks(a):
        return jnp.moveaxis(a.reshape((B, nblk, Q_BLOCK) + a.shape[2:]), 1, 0)

    def block(args):
        qb, qib, wib, t0 = args
        qpos = t0 + jnp.arange(Q_BLOCK)
        sc = index_scores(qib, wib, ki)
        sc = jnp.where(key_pos[None, None, :] <= qpos[None, :, None], sc, -jnp.inf)
        _, sel = lax.top_k(sc, topk)
        valid = sel <= qpos[None, :, None]
        return sparse_attend(qb, gather_rows(k, sel), gather_rows(v, sel), valid)

    out = lax.map(block, (to_blocks(q), to_blocks(qi), to_blocks(wi), jnp.arange(nblk) * Q_BLOCK))
    return jnp.moveaxis(out, 0, 1).reshape(q.shape)


def attend_sample(q, k, v, qi, ki, wi, cache_k, cache_v, cache_kidx, page_table):
    B, T = q.shape[:2]
    n_pages = page_table.shape[1]
    P = n_pages * PAGE_SIZE
    L = P + T
    topk = min(TOPK_MAX, L // 4)
    ki_past = cache_kidx[page_table].reshape(B, P, IDX_DIM).astype(ki.dtype)
    ki_all = jnp.concatenate([ki_past, ki], axis=1)
    qpos = P + jnp.arange(T)
    sc = index_scores(qi, wi, ki_all)
    sc = jnp.where(jnp.arange(L)[None, None, :] <= qpos[None, :, None], sc, -jnp.inf)
    _, sel = lax.top_k(sc, topk)
    in_past = sel < P
    pidx = jnp.minimum(sel, P - 1)
    phys = jax.vmap(lambda pt, pg: pt[pg])(page_table, pidx // PAGE_SIZE)
    slot = pidx % PAGE_SIZE
    nidx = jnp.clip(sel - P, 0, T - 1)
    k_sel = jnp.where(in_past[..., None, None], cache_k[phys, slot].astype(k.dtype), gather_rows(k, nidx))
    v_sel = jnp.where(in_past[..., None, None], cache_v[phys, slot].astype(v.dtype), gather_rows(v, nidx))
    valid = sel <= qpos[None, :, None]
    return sparse_attend(q, k_sel, v_sel, valid)


def hybrid_layer(x, pos0, attend, lru_buf, lru_h0, ffn_buf, lw):
    (norm_mix_g, w_in, lru_conv_w, lru_conv_b, lru_wa, lru_ba, lru_wx, lru_bx, lru_lambda,
     w_proj_a, w_proj_b, w_out, norm_ffn_g, w_up, ffn_conv_w, ffn_conv_b, w_down) = lw
    B, T = x.shape[:2]
    pos = pos0 + jnp.arange(T)
    h = rmsnorm(x, norm_mix_g)
    q, k, v, qi, ki, wi, xl, gl, ga, gb = split_cols(h @ w_in, IN_SPLITS)
    q = rope(q.reshape(B, T, N_HEADS, HEAD_DIM), pos)
    k = rope(k.reshape(B, T, N_HEADS, HEAD_DIM), pos)
    v = v.reshape(B, T, N_HEADS, HEAD_DIM)
    qi = rope(qi.reshape(B, T, IDX_HEADS, IDX_DIM), pos)
    ki = rope(ki[:, :, None, :], pos)[:, :, 0]
    wi = wi * IDX_HEADS ** -0.5
    y_a = attend(q, k, v, qi, ki, wi).reshape(B, T, ATTN_WIDTH) @ w_proj_a
    xc, lru_buf_new = causal_dwconv(xl, lru_buf, lru_conv_w, lru_conv_b)
    hl, h_last = rg_lru(xc, lru_h0, lru_wa, lru_ba, lru_wx, lru_bx, lru_lambda)
    y_b = (hl * jax.nn.gelu(gl)) @ w_proj_b
    x = x + (jax.nn.sigmoid(ga) * y_a + jax.nn.sigmoid(gb) * y_b) @ w_out
    h = rmsnorm(x, norm_ffn_g)
    ua, ub = split_cols(h @ w_up, (D_FF, D_FF))
    uc, ffn_buf_new = causal_dwconv(ua, ffn_buf, ffn_conv_w, ffn_conv_b)
    x = x + (jax.nn.gelu(uc) * ub) @ w_down
    return x, k, v, ki, lru_buf_new, h_last, ffn_buf_new


def setup_inputs(seed: int = 0) -> dict:
    key = jax.random.key(seed)
    ks = iter(jax.random.split(key, 40))
    n_pages = PAST_LEN // PAGE_SIZE
    n_used = DEC_BATCH * n_pages
    n_phys = n_used + n_used // 4

    def nrm(shape, scale):
        return jax.random.normal(next(ks), shape, jnp.float32) * scale

    a_c = jax.random.uniform(next(ks), (DEPTH, LRU_WIDTH), jnp.float32, 0.9, 0.999)
    a0 = jnp.power(a_c, 1.0 / LRU_C)
    lru_lambda = jnp.log(a0) - jnp.log1p(-a0)
    return {
        "x_prompt": nrm((BATCH, SEQ, D_MODEL), 1.0),
        "x_sample": nrm((DEC_BATCH, DEC_SEQ, D_MODEL), 1.0),
        "cache_k": nrm((DEPTH, n_phys, PAGE_SIZE, N_HEADS, HEAD_DIM), 1.0),
        "cache_v": nrm((DEPTH, n_phys, PAGE_SIZE, N_HEADS, HEAD_DIM), 1.0),
        "cache_kidx": nrm((DEPTH, n_phys, PAGE_SIZE, IDX_DIM), 1.0),
        "page_table": jax.random.permutation(next(ks), n_phys)[:n_used].reshape(DEC_BATCH, n_pages).astype(jnp.int32),
        "state_lru_conv": nrm((DEPTH, DEC_BATCH, LRU_CONV_W - 1, LRU_WIDTH), 1.0),
        "state_lru_h": nrm((DEPTH, DEC_BATCH, LRU_WIDTH), 0.5),
        "state_ffn_conv": nrm((DEPTH, DEC_BATCH, FFN_CONV_W - 1, D_FF), 1.0),
        "norm_mix_g": 1.0 + nrm((DEPTH, D_MODEL), 0.02),
        "w_in": nrm((DEPTH, D_MODEL, D_IN), D_MODEL ** -0.5),
        "lru_conv_w": nrm((DEPTH, LRU_CONV_W, LRU_WIDTH), LRU_CONV_W ** -0.5),
        "lru_conv_b": nrm((DEPTH, LRU_WIDTH), 0.01),
        "lru_wa": nrm((DEPTH, LRU_BLOCKS, LRU_BW, LRU_BW), LRU_BW ** -0.5),
        "lru_ba": nrm((DEPTH, LRU_WIDTH), 0.01),
        "lru_wx": nrm((DEPTH, LRU_BLOCKS, LRU_BW, LRU_BW), LRU_BW ** -0.5),
        "lru_bx": nrm((DEPTH, LRU_WIDTH), 0.01),
        "lru_lambda": lru_lambda,
        "w_proj_a": nrm((DEPTH, ATTN_WIDTH, D_MODEL), ATTN_WIDTH ** -0.5),
        "w_proj_b": nrm((DEPTH, LRU_WIDTH, D_MODEL), LRU_WIDTH ** -0.5),
        "w_out": nrm((DEPTH, D_MODEL, D_MODEL), D_MODEL ** -0.5),
        "norm_ffn_g": 1.0 + nrm((DEPTH, D_MODEL), 0.02),
        "w_up": nrm((DEPTH, D_MODEL, 2 * D_FF), D_MODEL ** -0.5),
        "ffn_conv_w": nrm((DEPTH, FFN_CONV_W, D_FF), FFN_CONV_W ** -0.5),
        "ffn_conv_b": nrm((DEPTH, D_FF), 0.01),
        "w_down": nrm((DEPTH, D_FF, D_MODEL), D_FF ** -0.5),
        "norm_final_g": 1.0 + nrm((D_MODEL,), 0.02),
    }


def reference(x_prompt, x_sample, cache_k, cache_v, cache_kidx, page_table, state_lru_conv, state_lru_h,
              state_ffn_conv, norm_mix_g, w_in, lru_conv_w, lru_conv_b, lru_wa, lru_ba, lru_wx, lru_bx,
              lru_lambda, w_proj_a, w_proj_b, w_out, norm_ffn_g, w_up, ffn_conv_w, ffn_conv_b, w_down,
              norm_final_g):
    B = x_prompt.shape[0]
    past_len = page_table.shape[1] * PAGE_SIZE
    xp, xs = x_prompt, x_sample
    st_p, st_s = [], []
    for l in range(DEPTH):
        lw = (norm_mix_g[l], w_in[l], lru_conv_w[l], lru_conv_b[l], lru_wa[l], lru_ba[l], lru_wx[l], lru_bx[l],
              lru_lambda[l], w_proj_a[l], w_proj_b[l], w_out[l], norm_ffn_g[l], w_up[l], ffn_conv_w[l],
              ffn_conv_b[l], w_down[l])
        xp, *sp = hybrid_layer(
            xp, 0, attend_prompt,
            jnp.zeros((B, LRU_CONV_W - 1, LRU_WIDTH), xp.dtype),
            jnp.zeros((B, LRU_WIDTH), xp.dtype),
            jnp.zeros((B, FFN_CONV_W - 1, D_FF), xp.dtype), lw)
        att_s = functools.partial(attend_sample, cache_k=cache_k[l], cache_v=cache_v[l],
                                  cache_kidx=cache_kidx[l], page_table=page_table)
        xs, *ss = hybrid_layer(xs, past_len, att_s, state_lru_conv[l], state_lru_h[l], state_ffn_conv[l], lw)
        st_p.append(sp)
        st_s.append(ss)
    y_prompt = rmsnorm(xp, norm_final_g)
    y_sample = rmsnorm(xs, norm_final_g)
    k_p, v_p, ki_p, lc_p, lh_p, fc_p = [jnp.stack(a) for a in zip(*st_p)]
    k_s, v_s, ki_s, lc_s, lh_s, fc_s = [jnp.stack(a) for a in zip(*st_s)]
    return (y_prompt, y_sample, k_p, v_p, ki_p, lc_p, lh_p, fc_p, k_s, v_s, ki_s, lc_s, lh_s, fc_s)
```

```python
import functools

import numpy as np
import jax
import jax.numpy as jnp
from jax import lax
from jax.experimental import pallas as pl
from jax.experimental.pallas import tpu as pltpu

F32 = jnp.float32
BF16 = jnp.bfloat16
I32 = jnp.int32

N_HEADS = 8
HEAD_DIM = 64
ATTN_WIDTH = N_HEADS * HEAD_DIM
IDX_HEADS = 8
IDX_DIM = 64
TOPK = 256
LRU_WIDTH = 512
LRU_BLOCKS = 8
LRU_CONV_W = 4
LRU_C = 8.0
FFN_CONV_W = 3
ROPE_THETA = 10000.0
EPS = 1e-6
PAGE_SIZE = 128

LANES = 128
INT_MIN = -2 ** 31
NEG_BIG = -1e30
VMEM_LIMIT = 56 * 1024 * 1024

NT_DIMS = (((1,), (1,)), ((), ()))


def _cparams(n_axes):
    return pltpu.CompilerParams(dimension_semantics=("arbitrary",) * n_axes,
                                vmem_limit_bytes=VMEM_LIMIT)


def _rms(x, g):
    r = lax.rsqrt(jnp.mean(x * x, axis=-1, keepdims=True) + EPS)
    return x * r * g


def _gelu(x):
    c = np.float32(np.sqrt(2.0 / np.pi))
    return x * (0.5 * (1.0 + jnp.tanh(c * (x + np.float32(0.044715) * (x * x * x)))))


def _sigmoid(x):
    return 1.0 / (1.0 + jnp.exp(-x))


def _softplus(z):
    return jnp.maximum(z, 0.0) + jnp.log(1.0 + jnp.exp(-jnp.abs(z)))


def _order_key(x):
    x = jnp.where(x == 0.0, 0.0, x)
    bits = pltpu.bitcast(x, I32)
    return bits ^ ((bits >> 31) & jnp.int32(0x7FFFFFFF))


def _qkv_kernel(x_ref, g_ref, wbig_ref, wsm_ref, wvt_ref, cos_ref, sa_ref, sb_ref, *out_refs, prompt):
    if prompt:
        q_ref, qi_ref, kf_ref, kb_ref, vf_ref, vt_ref, kiwi_ref, ki2_ref = out_refs
    else:
        q_ref, qi_ref, kf_ref, vf_ref, kiwi_ref = out_refs
    h = _rms(x_ref[...], g_ref[...]).astype(BF16)
    y = jnp.dot(h, wbig_ref[...], preferred_element_type=F32)
    ys = jnp.dot(h, wsm_ref[...], preferred_element_type=F32)
    cos, sa, sb = cos_ref[...], sa_ref[...], sb_ref[...]

    def rope(t):
        return t * cos + pltpu.roll(t, 96, 1) * sa + pltpu.roll(t, 32, 1) * sb

    for j in range(ATTN_WIDTH // LANES):
        sl = slice(LANES * j, LANES * (j + 1))
        qj = rope(y[:, LANES * j:LANES * (j + 1)]) * 0.125
        q_ref[:, sl] = qj.astype(q_ref.dtype)
        kj = rope(y[:, ATTN_WIDTH + LANES * j:ATTN_WIDTH + LANES * (j + 1)])
        kf_ref[:, sl] = kj
        if prompt:
            kb_ref[:, sl] = kj.astype(BF16)
        qij = rope(y[:, 3 * ATTN_WIDTH + LANES * j:3 * ATTN_WIDTH + LANES * (j + 1)]) * 0.125
        qi_ref[:, sl] = qij.astype(BF16)
    vf_ref[...] = y[:, 2 * ATTN_WIDTH:3 * ATTN_WIDTH]
    ysr = rope(ys)
    lane = lax.broadcasted_iota(I32, ys.shape, 1)
    kiwi_ref[...] = jnp.where(lane < IDX_DIM, ysr, ys * np.float32(IDX_HEADS ** -0.5))
    if prompt:
        ki2_ref[...] = jnp.where(lane < IDX_DIM, ysr, pltpu.roll(ysr, IDX_DIM, 1)).astype(BF16)
        vt_ref[0] = lax.dot_general(wvt_ref[...], h, NT_DIMS,
                                    preferred_element_type=F32).astype(BF16)


def _qkv_call(x2d, g, wbig, wsm, wvt, cos, sa, sb, *, tm, seq, prompt):
    n, d = x2d.shape
    nt = n // tm
    tps = seq // tm if prompt else 1
    nb = n // seq if prompt else 1
    tok = lambda w: pl.BlockSpec((tm, w), lambda i: (i, 0))
    const = lambda a: pl.BlockSpec(a.shape, lambda i: (0,) * a.ndim)
    tab = pl.BlockSpec((tm, LANES), lambda i: (i % tps, 0))
    if prompt:
        out_shape = (jax.ShapeDtypeStruct((n, ATTN_WIDTH), BF16),
                     jax.ShapeDtypeStruct((n, ATTN_WIDTH), BF16),
                     jax.ShapeDtypeStruct((n, ATTN_WIDTH), F32),
                     jax.ShapeDtypeStruct((n, ATTN_WIDTH), BF16),
                     jax.ShapeDtypeStruct((n, ATTN_WIDTH), F32),
                     jax.ShapeDtypeStruct((nb, ATTN_WIDTH, seq), BF16),
                     jax.ShapeDtypeStruct((n, LANES), F32),
                     jax.ShapeDtypeStruct((n, LANES), BF16))
        out_specs = (tok(ATTN_WIDTH), tok(ATTN_WIDTH), tok(ATTN_WIDTH), tok(ATTN_WIDTH), tok(ATTN_WIDTH),
                     pl.BlockSpec((1, ATTN_WIDTH, tm), lambda i: (i // tps, 0, i % tps)),
                     tok(LANES), tok(LANES))
    else:
        out_shape = (jax.ShapeDtypeStruct((n, ATTN_WIDTH), F32),
                     jax.ShapeDtypeStruct((n, ATTN_WIDTH), BF16),
                     jax.ShapeDtypeStruct((n, ATTN_WIDTH), F32),
                     jax.ShapeDtypeStruct((n, ATTN_WIDTH), F32),
                     jax.ShapeDtypeStruct((n, LANES), F32))
        out_specs = (tok(ATTN_WIDTH), tok(ATTN_WIDTH), tok(ATTN_WIDTH), tok(ATTN_WIDTH), tok(LANES))
    return pl.pallas_call(
        functools.partial(_qkv_kernel, prompt=prompt),
        grid=(nt,),
        in_specs=[tok(d), const(g), const(wbig), const(wsm), const(wvt), tab, tab, tab],
        out_specs=out_specs, out_shape=out_shape,
        compiler_params=_cparams(1),
        name="qkv_prompt" if prompt else "qkv_sample",
    )(x2d, g, wbig, wsm, wvt, cos, sa, sb)


def _lru_gates(xc, wa_ref, ba_ref, wx_ref, bx_ref, lam_ref):
    xcb = xc.astype(BF16)
    r = _sigmoid(jnp.dot(xcb, wa_ref[...], preferred_element_type=F32) + ba_ref[...])
    i = _sigmoid(jnp.dot(xcb, wx_ref[...], preferred_element_type=F32) + bx_ref[...])
    log_a = -LRU_C * r * _softplus(-lam_ref[...])
    a = jnp.exp(log_a)
    mult = jnp.sqrt(1.0 - jnp.exp(2.0 * log_a))
    return a, mult * (i * xc)


def _lru_prompt_kernel(x_ref, g_ref, wxl_ref, wgl_ref, cw_ref, cb_ref, wa_ref, ba_ref, wx_ref, bx_ref,
                       lam_ref, y_ref, conv_ref, hlast_ref, xs_ref, a_ref, b_ref, hs_ref, hc_ref, *, tc):
    t = pl.program_id(1)

    @pl.when(t == 0)
    def _():
        xs_ref[0:8, :] = jnp.zeros((8, LRU_WIDTH), F32)
        hc_ref[...] = jnp.zeros(hc_ref.shape, F32)

    h = _rms(x_ref[0], g_ref[...]).astype(BF16)
    xl = jnp.dot(h, wxl_ref[...], preferred_element_type=F32)
    gl = jnp.dot(h, wgl_ref[...], preferred_element_type=F32)
    xs_ref[8:8 + tc, :] = xl
    cw = cw_ref[...]
    xc = cb_ref[...] + xs_ref[5:5 + tc, :] * cw[0:1]
    xc = xc + xs_ref[6:6 + tc, :] * cw[1:2]
    xc = xc + xs_ref[7:7 + tc, :] * cw[2:3]
    xc = xc + xl * cw[3:4]
    tail = xl[tc - 8:tc, :]
    xs_ref[0:8, :] = tail
    conv_ref[0] = tail[8 - (LRU_CONV_W - 1):8, :]

    a, bt = _lru_gates(xc, wa_ref, ba_ref, wx_ref, bx_ref, lam_ref)
    a_ref[...] = a
    b_ref[...] = bt

    def step(i, hp):
        hn = a_ref[pl.ds(i, 1), :] * hp + b_ref[pl.ds(i, 1), :]
        hs_ref[pl.ds(i, 1), :] = hn
        return hn

    hl = lax.fori_loop(0, tc, step, hc_ref[...], unroll=8)
    hc_ref[...] = hl
    hlast_ref[0] = hl
    y_ref[0] = (hs_ref[...] * _gelu(gl)).astype(BF16)


def _lru_prompt_call(x3d, g, wxl, wgl, cw, cb, wa, ba, wx, bx, lam, *, tc):
    nb, seq, d = x3d.shape
    const = lambda a: pl.BlockSpec(a.shape, lambda b, t: (0,) * a.ndim)
    return pl.pallas_call(
        functools.partial(_lru_prompt_kernel, tc=tc),
        grid=(nb, seq // tc),
        in_specs=[pl.BlockSpec((1, tc, d), lambda b, t: (b, t, 0))] +
                 [const(a) for a in (g, wxl, wgl, cw, cb, wa, ba, wx, bx, lam)],
        out_specs=(pl.BlockSpec((1, tc, LRU_WIDTH), lambda b, t: (b, t, 0)),
                   pl.BlockSpec((1, LRU_CONV_W - 1, LRU_WIDTH), lambda b, t: (b, 0, 0)),
                   pl.BlockSpec((1, 1, LRU_WIDTH), lambda b, t: (b, 0, 0))),
        out_shape=(jax.ShapeDtypeStruct((nb, seq, LRU_WIDTH), BF16),
                   jax.ShapeDtypeStruct((nb, LRU_CONV_W - 1, LRU_WIDTH), F32),
                   jax.ShapeDtypeStruct((nb, 1, LRU_WIDTH), F32)),
        scratch_shapes=[pltpu.VMEM((tc + 8, LRU_WIDTH), F32), pltpu.VMEM((tc, LRU_WIDTH), F32),
                        pltpu.VMEM((tc, LRU_WIDTH), F32), pltpu.VMEM((tc, LRU_WIDTH), F32),
                        pltpu.VMEM((1, LRU_WIDTH), F32)],
        compiler_params=_cparams(2),
        name="lru_prompt",
    )(x3d, g, wxl, wgl, cw, cb, wa, ba, wx, bx, lam)


def _lru_sample_kernel(x_ref, g_ref, wxl_ref, wgl_ref, cw_ref, cb_ref, wa_ref, ba_ref, wx_ref, bx_ref,
                       lam_ref, buf_ref, h0_ref, y_ref, conv_ref, hlast_ref, *, nb, nt):
    h = _rms(x_ref[...], g_ref[...]).astype(BF16)
    xl = jnp.dot(h, wxl_ref[...], preferred_element_type=F32)
    gl = jnp.dot(h, wgl_ref[...], preferred_element_type=F32)
    xx = jnp.concatenate([buf_ref[...], xl], axis=0)
    cw = cw_ref[...]
    n = nb * nt
    xc = cb_ref[...] + xx[0:n] * cw[0:1]
    for j in range(1, LRU_CONV_W):
        xc = xc + xx[j * nb:j * nb + n] * cw[j:j + 1]
    conv_ref[...] = xx[n:n + (LRU_CONV_W - 1) * nb]
    a, bt = _lru_gates(xc, wa_ref, ba_ref, wx_ref, bx_ref, lam_ref)
    hp = h0_ref[...]
    hs = []
    for t in range(nt):
        hp = a[t * nb:(t + 1) * nb] * hp + bt[t * nb:(t + 1) * nb]
        hs.append(hp)
    hlast_ref[...] = hp
    y_ref[...] = (jnp.concatenate(hs, axis=0) * _gelu(gl)).astype(BF16)


def _lru_sample_call(x2d, g, wxl, wgl, cw, cb, wa, ba, wx, bx, lam, buf, h0, *, nb, nt):
    n = nb * nt
    return pl.pallas_call(
        functools.partial(_lru_sample_kernel, nb=nb, nt=nt),
        out_shape=(jax.ShapeDtypeStruct((n, LRU_WIDTH), BF16),
                   jax.ShapeDtypeStruct(((LRU_CONV_W - 1) * nb, LRU_WIDTH), F32),
                   jax.ShapeDtypeStruct((nb, LRU_WIDTH), F32)),
        compiler_params=pltpu.CompilerParams(vmem_limit_bytes=VMEM_LIMIT),
        name="lru_sample",
    )(x2d, g, wxl, wgl, cw, cb, wa, ba, wx, bx, lam, buf, h0)


def _count_rows(m):
    r, c = m.shape
    part = jnp.sum(jnp.where(m, 1.0, 0.0).reshape(r // 8, 8, c), axis=0)
    return jnp.sum(part, axis=0, keepdims=True)


def _attn_prompt_kernel(qi_ref, wit_ref, ki2_ref, q_ref, k_ref, vt_ref, o_ref, key_ref, bias_ref, *, tq, tk):
    qb = pl.program_id(1)
    nk = qb + 1
    half = lax.broadcasted_iota(I32, (tq, LANES), 1) // HEAD_DIM
    kpos0 = lax.broadcasted_iota(I32, (tk, tq), 0)
    qpos = qb * tq + lax.broadcasted_iota(I32, (tk, tq), 1)

    def masked_pair(ref, h):
        pair = ref[0, :, LANES * (h // 2):LANES * (h // 2 + 1)]
        return jnp.where(half == (h % 2), pair, jnp.zeros_like(pair))

    qim = [masked_pair(qi_ref, h) for h in range(IDX_HEADS)]
    wit = wit_ref[0]

    def score_chunk(c, carry):
        off = pl.multiple_of(c * tk, tk)
        kc = ki2_ref[0, pl.ds(off, tk), :]
        acc = jnp.zeros((tk, tq), F32)
        for h in range(IDX_HEADS):
            s = lax.dot_general(kc, qim[h], NT_DIMS, preferred_element_type=F32)
            acc = acc + jnp.maximum(s, 0.0) * wit[h:h + 1, :]
        key = jnp.where(kpos0 + off <= qpos, _order_key(acc), jnp.int32(INT_MIN))
        key_ref[pl.ds(off, tk), :] = key
        return carry

    lax.fori_loop(0, nk, score_chunk, 0)

    def count_ge(trial):
        def body(c, cnt):
            off = pl.multiple_of(c * tk, tk)
            return cnt + _count_rows(key_ref[pl.ds(off, tk), :] >= trial)
        return lax.fori_loop(0, nk, body, jnp.zeros((1, tq), F32))

    def bisect(i, t):
        trial = t + lax.shift_left(jnp.int32(1), jnp.int32(31) - i)
        return jnp.where(count_ge(trial) >= float(TOPK), trial, t)

    t = lax.fori_loop(0, 32, bisect, jnp.full((1, tq), INT_MIN, I32))
    t = jnp.maximum(t, jnp.int32(INT_MIN + 1))
    n_ge = count_ge(t)
    has_ties = jnp.max(n_ge) > float(TOPK)

    def write_bias_simple():
        def body(c, carry):
            off = pl.multiple_of(c * tk, tk)
            bias_ref[pl.ds(off, tk), :] = jnp.where(key_ref[pl.ds(off, tk), :] >= t, 0.0, NEG_BIG)
            return carry
        lax.fori_loop(0, nk, body, 0)

    def write_bias_ties():
        def count_gt(c, cnt):
            off = pl.multiple_of(c * tk, tk)
            return cnt + _count_rows(key_ref[pl.ds(off, tk), :] > t)
        need = float(TOPK) - lax.fori_loop(0, nk, count_gt, jnp.zeros((1, tq), F32))

        def count_ties_below(jt):
            def body(c, cnt):
                off = pl.multiple_of(c * tk, tk)
                kc = key_ref[pl.ds(off, tk), :]
                return cnt + _count_rows(jnp.where(kpos0 + off < jt, kc, jnp.int32(INT_MIN)) == t)
            return lax.fori_loop(0, nk, body, jnp.zeros((1, tq), F32))

        def bis(i, j):
            jt = j + lax.shift_left(jnp.int32(1), jnp.int32(15) - i)
            return jnp.where(count_ties_below(jt) <= need, jt, j)

        jlim = lax.fori_loop(0, 16, bis, jnp.zeros((1, tq), I32))

        def body(c, carry):
            off = pl.multiple_of(c * tk, tk)
            kc = key_ref[pl.ds(off, tk), :]
            keep = kc >= jnp.where(kpos0 + off < jlim, t, t + 1)
            bias_ref[pl.ds(off, tk), :] = jnp.where(keep, 0.0, NEG_BIG)
            return carry
        lax.fori_loop(0, nk, body, 0)

    lax.cond(has_ties, write_bias_ties, write_bias_simple)

    outs = []
    for j in range(N_HEADS // 2):
        qm = [masked_pair(q_ref, 2 * j), masked_pair(q_ref, 2 * j + 1)]

        def chunk(c, carry, j=j, qm=qm):
            off = pl.multiple_of(c * tk, tk)
            kc = k_ref[0, pl.ds(off, tk), LANES * j:LANES * (j + 1)]
            bias = bias_ref[pl.ds(off, tk), :]
            new = []
            for e in range(2):
                m, l, acc = carry[3 * e:3 * e + 3]
                s = lax.dot_general(kc, qm[e], NT_DIMS, preferred_element_type=F32) + bias
                m_new = jnp.maximum(m, jnp.max(s, axis=0, keepdims=True))
                p = jnp.exp(s - m_new)
                alpha = jnp.exp(m - m_new)
                l_new = alpha * l + jnp.sum(p, axis=0, keepdims=True)
                r0 = LANES * j + HEAD_DIM * e
                vt = vt_ref[0, r0:r0 + HEAD_DIM, pl.ds(off, tk)]
                acc_new = alpha * acc + jnp.dot(vt, p.astype(BF16), preferred_element_type=F32)
                new += [m_new, l_new, acc_new]
            return tuple(new)

        init = (jnp.full((1, tq), NEG_BIG, F32), jnp.zeros((1, tq), F32), jnp.zeros((HEAD_DIM, tq), F32)) * 2
        res = lax.fori_loop(0, nk, chunk, init)
        for e in range(2):
            outs.append(res[3 * e + 2] / res[3 * e + 1])
    o_t = jnp.concatenate(outs, axis=0).astype(BF16)
    eye = (lax.broadcasted_iota(I32, (tq, tq), 0) == lax.broadcasted_iota(I32, (tq, tq), 1))
    eye = jnp.where(eye, 1.0, 0.0).astype(BF16)
    o_ref[0] = lax.dot_general(eye, o_t, NT_DIMS, preferred_element_type=F32).astype(BF16)


def _attn_prompt_call(qi, wit, ki2, q, k, vt, *, tq):
    nb, seq, _ = q.shape
    return pl.pallas_call(
        functools.partial(_attn_prompt_kernel, tq=tq, tk=tq),
        grid=(nb, seq // tq),
        in_specs=[pl.BlockSpec((1, tq, ATTN_WIDTH), lambda b, i: (b, i, 0)),
                  pl.BlockSpec((1, IDX_HEADS, tq), lambda b, i: (b, 0, i)),
                  pl.BlockSpec((1, seq, LANES), lambda b, i: (b, 0, 0)),
                  pl.BlockSpec((1, tq, ATTN_WIDTH), lambda b, i: (b, i, 0)),
                  pl.BlockSpec((1, seq, ATTN_WIDTH), lambda b, i: (b, 0, 0)),
                  pl.BlockSpec((1, ATTN_WIDTH, seq), lambda b, i: (b, 0, 0))],
        out_specs=pl.BlockSpec((1, tq, ATTN_WIDTH), lambda b, i: (b, i, 0)),
        out_shape=jax.ShapeDtypeStruct((nb, seq, ATTN_WIDTH), BF16),
        scratch_shapes=[pltpu.VMEM((seq, tq), I32), pltpu.VMEM((seq, tq), F32)],
        compiler_params=_cparams(2),
        name="attn_prompt",
    )(qi, wit, ki2, q, k, vt)


N_CHUNK_ROWS = 256


def _select_kernel(pt_ref, qi_ref, wi_ref, kinew_ref, ptcol_ref, kidx_hbm, idx_ref, newsel_ref,
                   kbuf, sem, sc_ref, *, n_pages, nt, group):
    b = pl.program_id(0)
    nb = pl.num_programs(0)
    slot = b % 2
    n_keys = n_pages * PAGE_SIZE

    def page_copy(bb, sl, p):
        return pltpu.make_async_copy(kidx_hbm.at[pt_ref[bb, p]],
                                     kbuf.at[sl, pl.ds(pl.multiple_of(p * PAGE_SIZE, PAGE_SIZE), PAGE_SIZE)],
                                     sem.at[sl])

    def fetch(bb, sl):
        def body(p, carry):
            page_copy(bb, sl, p).start()
            return carry
        lax.fori_loop(0, n_pages, body, 0)

    @pl.when(b == 0)
    def _():
        fetch(0, 0)

    @pl.when(b + 1 < nb)
    def _():
        fetch(b + 1, 1 - slot)

    def wait_body(p, carry):
        page_copy(b, slot, p).wait()
        return carry
    lax.fori_loop(0, n_pages, wait_body, 0)

    qi = qi_ref[0]
    wi = wi_ref[0]
    n_rows = IDX_HEADS * nt

    def head_sum(s):
        e = s[0:8]
        for r in range(1, n_rows // 8):
            e = e + s[8 * r:8 * r + 8]
        return e[0:nt] + e[nt:2 * nt]

    wi_g = jnp.concatenate([wi] * (group // LANES), axis=1)

    def score_group(gi, carry):
        off = pl.multiple_of(gi * group, group)
        kc = kbuf[slot, pl.ds(off, group), :].astype(BF16)
        s = lax.dot_general(qi, kc, NT_DIMS, preferred_element_type=F32)
        key = _order_key(head_sum(jnp.maximum(s, 0.0) * wi_g))
        row0 = gi * (group // LANES)
        for q in range(nt):
            for j in range(group // LANES):
                sc_ref[q, pl.ds(row0 + j, 1), :] = key[q:q + 1, LANES * j:LANES * (j + 1)]
        return carry

    lax.fori_loop(0, n_keys // group, score_group, 0)

    s_new = lax.dot_general(qi, kinew_ref[0], NT_DIMS, preferred_element_type=F32)
    key_new = _order_key(head_sum(jnp.maximum(s_new, 0.0) * wi))
    lane = lax.broadcasted_iota(I32, (nt, LANES), 1)
    qrow = lax.broadcasted_iota(I32, (nt, LANES), 0)
    key_new = jnp.where(lane <= qrow, key_new, jnp.int32(INT_MIN))
    for q in range(nt):
        sc_ref[q, n_pages:n_pages + 1, :] = key_new[q:q + 1, :]
        sc_ref[q, n_pages + 1:n_pages + 8, :] = jnp.full((7, LANES), INT_MIN, I32)
        sc_ref[q, n_pages + 8:N_CHUNK_ROWS, :] = jnp.full((N_CHUNK_ROWS - n_pages - 8, LANES), INT_MIN, I32)

    def count3(m):
        part = jnp.sum(jnp.where(m, 1.0, 0.0), axis=1, keepdims=True)
        return jnp.sum(part, axis=2, keepdims=True)

    def bisect(i, t):
        trial = t + lax.shift_left(jnp.int32(1), jnp.int32(31) - i)
        return jnp.where(count3(sc_ref[...] >= trial) >= float(TOPK), trial, t)

    t = lax.fori_loop(0, 32, bisect, jnp.full((nt, 1, 1), INT_MIN, I32))
    keys = sc_ref[...]
    need = float(TOPK) - count3(keys > t)
    pos = (lax.broadcasted_iota(I32, keys.shape, 1) * LANES + lax.broadcasted_iota(I32, keys.shape, 2))

    def bis(i, j):
        jt = j + lax.shift_left(jnp.int32(1), jnp.int32(16) - i)
        below = jnp.where(pos < jt, sc_ref[...], jnp.int32(INT_MIN)) == t
        return jnp.where(count3(below) <= need, jt, j)

    n_ge = count3(keys >= t)
    jlim = lax.cond(jnp.max(n_ge) > float(TOPK),
                    lambda: lax.fori_loop(0, 17, bis, jnp.zeros((nt, 1, 1), I32)),
                    lambda: jnp.full((nt, 1, 1), 2 ** 17 - 1, I32))
    sel = keys >= jnp.where(pos < jlim, t, t + 1)

    ri = lax.broadcasted_iota(I32, (N_CHUNK_ROWS, N_CHUNK_ROWS), 0)
    ci = lax.broadcasted_iota(I32, (N_CHUNK_ROWS, N_CHUNK_ROWS), 1)
    ltri_c = jnp.where(ci <= ri, 1.0, 0.0).astype(BF16)
    jrow = ci.astype(F32)
    ltri_k = ltri_c[0:LANES, 0:LANES]
    ones_k = jnp.ones((LANES, LANES), BF16)
    ptcol = ptcol_ref[0]
    for q in range(nt):
        m_bf = jnp.where(sel[q], 1.0, 0.0).astype(BF16)
        cnt = jnp.dot(m_bf, ones_k, preferred_element_type=F32)
        cum = jnp.dot(ltri_c, cnt.astype(BF16), preferred_element_type=F32)
        cum2 = jnp.concatenate([cum, cum], axis=1)
        cumx2 = cum2 - jnp.concatenate([cnt, cnt], axis=1)
        oh = jnp.where(cumx2 <= jrow, jnp.where(jrow < cum2, 1.0, 0.0), 0.0)
        phys = jnp.sum(oh * ptcol, axis=0, keepdims=True)
        cumx = jnp.sum(oh * cumx2, axis=0, keepdims=True)
        pc_t = lax.dot_general(ltri_k, m_bf, NT_DIMS, preferred_element_type=F32)
        pcg = jnp.dot(pc_t.astype(BF16), oh.astype(BF16),
                      preferred_element_type=F32)
        rank = jrow[0:1, :] - cumx
        slot_j = jnp.sum(jnp.where(pcg <= rank, 1.0, 0.0), axis=0, keepdims=True)
        idx_ref[0, q:q + 1, :] = (phys * float(PAGE_SIZE) + slot_j).astype(I32)
        newsel_ref[0, q:q + 1, :] = jnp.where(sel[q, n_pages:n_pages + 1, :], 1, 0).astype(I32)


def _select_call(page_table, qi_s, wi_s, kinew, ptcol, cache_kidx, *, nt):
    nb, n_pages = page_table.shape
    group = 1024
    grid_spec = pltpu.PrefetchScalarGridSpec(
        num_scalar_prefetch=1,
        grid=(nb,),
        in_specs=[pl.BlockSpec((1, IDX_HEADS * nt, IDX_DIM), lambda b, pt: (b, 0, 0)),
                  pl.BlockSpec((1, IDX_HEADS * nt, LANES), lambda b, pt: (b, 0, 0)),
                  pl.BlockSpec((1, LANES, IDX_DIM), lambda b, pt: (b, 0, 0)),
                  pl.BlockSpec((1, N_CHUNK_ROWS, 1), lambda b, pt: (b, 0, 0)),
                  pl.BlockSpec(memory_space=pl.ANY)],
        out_specs=(pl.BlockSpec((1, nt, TOPK), lambda b, pt: (b, 0, 0)),
                   pl.BlockSpec((1, nt, LANES), lambda b, pt: (b, 0, 0))),
        scratch_shapes=[pltpu.VMEM((2, n_pages * PAGE_SIZE, IDX_DIM), F32),
                        pltpu.SemaphoreType.DMA((2,)),
                        pltpu.VMEM((nt, N_CHUNK_ROWS, LANES), I32)])
    return pl.pallas_call(
        functools.partial(_select_kernel, n_pages=n_pages, nt=nt, group=group),
        grid_spec=grid_spec,
        out_shape=(jax.ShapeDtypeStruct((nb, nt, TOPK), I32),
                   jax.ShapeDtypeStruct((nb, nt, LANES), I32)),
        compiler_params=_cparams(1),
        name="select_sample",
    )(page_table, qi_s, wi_s, kinew, ptcol, cache_kidx)


def _gather_kernel(idx_ref, ns_ref, q_ref, knew_ref, vnew_ref, ck_hbm, cv_hbm, o_ref, kb, vb, sem, *, nt):
    n = pl.program_id(0)
    nn = pl.num_programs(0)
    slot = n % 2

    def fetch(row, sl):
        def body(j, carry):
            i = idx_ref[row, j]
            pltpu.make_async_copy(ck_hbm.at[i], kb.at[sl, j], sem.at[0, sl]).start()
            pltpu.make_async_copy(cv_hbm.at[i], vb.at[sl, j], sem.at[1, sl]).start()
            return carry
        lax.fori_loop(0, TOPK, body, 0, unroll=8)

    @pl.when(n == 0)
    def _():
        fetch(0, 0)

    @pl.when(n + 1 < nn)
    def _():
        fetch(n + 1, 1 - slot)

    def wait_body(j, carry):
        pltpu.make_async_copy(ck_hbm.at[0], kb.at[slot, j], sem.at[0, slot]).wait()
        pltpu.make_async_copy(cv_hbm.at[0], vb.at[slot, j], sem.at[1, slot]).wait()
        return carry
    lax.fori_loop(0, TOPK, wait_body, 0, unroll=8)

    q = q_ref[0]
    n_new = ns_ref[n, 0]
    for j in range(1, nt):
        n_new = n_new + ns_ref[n, j]
    n_valid = TOPK - n_new
    s = jnp.sum(kb[slot] * q[None], axis=-1, keepdims=True)
    jio = lax.broadcasted_iota(I32, s.shape, 0)
    s = jnp.where(jio < n_valid, s, NEG_BIG)
    sn = jnp.sum(knew_ref[0] * q[None], axis=-1, keepdims=True)
    nio = lax.broadcasted_iota(I32, sn.shape, 0)
    seln = jnp.zeros(sn.shape, I32)
    for j in range(nt):
        seln = jnp.where(nio == j, ns_ref[n, j], seln)
    sn = jnp.where(seln > 0, sn, NEG_BIG)
    m = jnp.maximum(jnp.max(s, axis=0, keepdims=True), jnp.max(sn, axis=0, keepdims=True))
    p = jnp.exp(s - m)
    pn = jnp.exp(sn - m)
    l = jnp.sum(p, axis=0) + jnp.sum(pn, axis=0)
    o = jnp.sum(p * vb[slot], axis=0) + jnp.sum(pn * vnew_ref[0], axis=0)
    o_ref[0] = o / l


def _gather_call(idx2d, newsel2d, q3, knew, vnew, ck_rows, cv_rows, *, nt):
    n = idx2d.shape[0]
    grid_spec = pltpu.PrefetchScalarGridSpec(
        num_scalar_prefetch=2,
        grid=(n,),
        in_specs=[pl.BlockSpec((1, N_HEADS, HEAD_DIM), lambda i, a, b: (i, 0, 0)),
                  pl.BlockSpec((1, nt, N_HEADS, HEAD_DIM), lambda i, a, b: (i // nt, 0, 0, 0)),
                  pl.BlockSpec((1, nt, N_HEADS, HEAD_DIM), lambda i, a, b: (i // nt, 0, 0, 0)),
                  pl.BlockSpec(memory_space=pl.ANY),
                  pl.BlockSpec(memory_space=pl.ANY)],
        out_specs=pl.BlockSpec((1, N_HEADS, HEAD_DIM), lambda i, a, b: (i, 0, 0)),
        scratch_shapes=[pltpu.VMEM((2, TOPK, N_HEADS, HEAD_DIM), F32),
                        pltpu.VMEM((2, TOPK, N_HEADS, HEAD_DIM), F32),
                        pltpu.SemaphoreType.DMA((2, 2))])
    return pl.pallas_call(
        functools.partial(_gather_kernel, nt=nt),
        grid_spec=grid_spec,
        out_shape=jax.ShapeDtypeStruct((n, N_HEADS, HEAD_DIM), F32),
        compiler_params=_cparams(1),
        name="gather_sample",
    )(idx2d, newsel2d, q3, knew, vnew, ck_rows, cv_rows)


def _ffn_kernel(x_ref, at_ref, yb_ref, gm_ref, wga_ref, wgb_ref, wpa_ref, wpb_ref, wo_ref, gf_ref,
                wua_ref, wub_ref, fcw_ref, fcb_ref, wd_ref, gfin_ref, buf_ref,
                y_ref, st_ref, x1_ref, h2_ref, acc_ref, us_ref, carry_ref, *, tm, fc, tps, sample_nb):
    i = pl.program_id(0)
    c = pl.program_id(1)
    nc = pl.num_programs(1)

    @pl.when(c == 0)
    def _():
        x = x_ref[...]
        h = _rms(x, gm_ref[...]).astype(BF16)
        ga = jnp.dot(h, wga_ref[...], preferred_element_type=F32)
        gb = jnp.dot(h, wgb_ref[...], preferred_element_type=F32)
        ya = jnp.dot(at_ref[...], wpa_ref[...], preferred_element_type=F32)
        yb = jnp.dot(yb_ref[...], wpb_ref[...], preferred_element_type=F32)
        mix = _sigmoid(ga) * ya + _sigmoid(gb) * yb
        x1 = x + jnp.dot(mix.astype(BF16), wo_ref[...], preferred_element_type=F32)
        x1_ref[...] = x1
        h2_ref[...] = _rms(x1, gf_ref[...]).astype(BF16)
        acc_ref[...] = jnp.zeros(acc_ref.shape, F32)

    h2 = h2_ref[...]
    ua = jnp.dot(h2, wua_ref[...], preferred_element_type=F32)
    ub = jnp.dot(h2, wub_ref[...], preferred_element_type=F32)
    w = fcw_ref[...]
    if sample_nb:
        nb = sample_nb
        us = jnp.concatenate([buf_ref[...], ua], axis=0)
        uc = fcb_ref[...] + us[0:tm] * w[0:1]
        for j in range(1, FFN_CONV_W):
            uc = uc + us[j * nb:j * nb + tm] * w[j:j + 1]
        st_ref[...] = us[tm:tm + (FFN_CONV_W - 1) * nb]
    else:
        first = (i % tps) == 0
        us_ref[0:8, :] = jnp.where(first, jnp.zeros((8, fc), F32), carry_ref[c])
        us_ref[8:8 + tm, :] = ua
        uc = fcb_ref[...] + us_ref[6:6 + tm, :] * w[0:1]
        uc = uc + us_ref[7:7 + tm, :] * w[1:2]
        uc = uc + ua * w[2:3]
        tail = ua[tm - 8:tm, :]
        carry_ref[c] = tail
        st_ref[0] = tail
    act = (_gelu(uc) * ub).astype(BF16)
    acc_ref[...] += jnp.dot(act, wd_ref[...], preferred_element_type=F32)

    @pl.when(c == nc - 1)
    def _():
        y_ref[...] = _rms(x1_ref[...] + acc_ref[...], gfin_ref[...])


def _ffn_call(x2d, attn, yb, gm, wga, wgb, wpa, wpb, wo, gf, wup, fcw, fcb, wd, gfin, buf, *,
              tm, fc, seq, sample_nb):
    n, d = x2d.shape
    d_ff = wd.shape[0]
    nc = d_ff // fc
    tps = seq // tm if not sample_nb else 1
    tok = lambda w: pl.BlockSpec((tm, w), lambda i, c: (i, 0))
    const = lambda a: pl.BlockSpec(a.shape, lambda i, c: (0,) * a.ndim)
    if sample_nb:
        nst = (FFN_CONV_W - 1) * sample_nb
        buf_spec = pl.BlockSpec((nst, fc), lambda i, c: (0, c))
        st_spec = pl.BlockSpec((nst, fc), lambda i, c: (0, c))
        st_shape = jax.ShapeDtypeStruct((nst, d_ff), F32)
    else:
        buf_spec = pl.BlockSpec((8, LANES), lambda i, c: (0, 0))
        st_spec = pl.BlockSpec((1, 8, fc), lambda i, c: (i, 0, c))
        st_shape = jax.ShapeDtypeStruct((n // tm, 8, d_ff), F32)
    return pl.pallas_call(
        functools.partial(_ffn_kernel, tm=tm, fc=fc, tps=tps, sample_nb=sample_nb),
        grid=(n // tm, nc),
        in_specs=[tok(d), tok(ATTN_WIDTH), tok(LRU_WIDTH), const(gm), const(wga), const(wgb), const(wpa),
                  const(wpb), const(wo), const(gf),
                  pl.BlockSpec((d, fc), lambda i, c: (0, c)),
                  pl.BlockSpec((d, fc), lambda i, c: (0, nc + c)),
                  pl.BlockSpec((FFN_CONV_W, fc), lambda i, c: (0, c)),
                  pl.BlockSpec((1, fc), lambda i, c: (0, c)),
                  pl.BlockSpec((fc, d), lambda i, c: (c, 0)),
                  const(gfin), buf_spec],
        out_specs=(tok(d), st_spec),
        out_shape=(jax.ShapeDtypeStruct((n, d), F32), st_shape),
        scratch_shapes=[pltpu.VMEM((tm, d), F32), pltpu.VMEM((tm, d), BF16), pltpu.VMEM((tm, d), F32),
                        pltpu.VMEM((tm + 8, fc), F32), pltpu.VMEM((nc, 8, fc), F32)],
        compiler_params=_cparams(2),
        name="ffn_sample" if sample_nb else "ffn_prompt",
    )(x2d, attn, yb, gm, wga, wgb, wpa, wpb, wo, gf, wup, wup, fcw, fcb, wd, gfin, buf)


def _rope_tables(pos):
    half = HEAD_DIM // 2
    inv = jnp.power(ROPE_THETA, -jnp.arange(half, dtype=F32) / half)
    ang = pos.astype(F32)[:, None] * inv[None, :]
    cos, sin = jnp.cos(ang), jnp.sin(ang)
    z = jnp.zeros_like(sin)
    tile = lambda a, b: jnp.concatenate([a, b, a, b], axis=1)
    return tile(cos, cos), tile(-sin, z), tile(z, sin)


def _block_diag(w):
    nblk, bw, _ = w.shape
    eye = jnp.eye(nblk, dtype=w.dtype)
    return (eye[:, None, :, None] * w[:, :, None, :]).reshape(nblk * bw, nblk * bw)


def kernel(x_prompt, x_sample, cache_k, cache_v, cache_kidx, page_table, state_lru_conv, state_lru_h,
           state_ffn_conv, norm_mix_g, w_in, lru_conv_w, lru_conv_b, lru_wa, lru_ba, lru_wx, lru_bx,
           lru_lambda, w_proj_a, w_proj_b, w_out, norm_ffn_g, w_up, ffn_conv_w, ffn_conv_b, w_down,
           norm_final_g):
    depth = w_in.shape[0]
    assert depth == 1, "single-layer step"
    nbp, seq, d = x_prompt.shape
    nbs, nts, _ = x_sample.shape
    n_pages = page_table.shape[1]
    past = n_pages * PAGE_SIZE
    d_ff = w_down.shape[1]
    a = ATTN_WIDTH
    row = lambda v: v.reshape(1, -1)

    win = w_in[0]
    o_ki = 4 * a
    o_wi = o_ki + IDX_DIM
    o_xl = o_wi + IDX_HEADS
    o_gl = o_xl + LRU_WIDTH
    o_ga = o_gl + LRU_WIDTH
    o_gb = o_ga + d
    wbig = win[:, :4 * a].astype(BF16)
    wsm = jnp.pad(win[:, o_ki:o_xl], ((0, 0), (0, LANES - IDX_DIM - IDX_HEADS))).astype(BF16)
    wvt = win[:, 2 * a:3 * a].T.astype(BF16)
    wxl = win[:, o_xl:o_gl].astype(BF16)
    wgl = win[:, o_gl:o_ga].astype(BF16)
    wga = win[:, o_ga:o_gb].astype(BF16)
    wgb = win[:, o_gb:o_gb + d].astype(BF16)
    wa_bd = _block_diag(lru_wa[0]).astype(BF16)
    wx_bd = _block_diag(lru_wx[0]).astype(BF16)
    wpa = w_proj_a[0].astype(BF16)
    wpb = w_proj_b[0].astype(BF16)
    wo = w_out[0].astype(BF16)
    wup = w_up[0].astype(BF16)
    wd = w_down[0].astype(BF16)
    gm, gf, gfin = row(norm_mix_g[0]), row(norm_ffn_g[0]), row(norm_final_g)
    lru_args = (lru_conv_w[0], row(lru_conv_b[0]), wa_bd, row(lru_ba[0]), wx_bd, row(lru_bx[0]),
                row(lru_lambda[0]))
    ffn_w = (gm, wga, wgb, wpa, wpb, wo, gf, wup, ffn_conv_w[0], row(ffn_conv_b[0]), wd, gfin)

    xp2 = x_prompt.reshape(nbp * seq, d)
    cos_p, sa_p, sb_p = _rope_tables(jnp.arange(seq, dtype=I32))
    q_p, qi_p, kf_p, kb_p, vf_p, vt_p, kiwi_p, ki2_p = _qkv_call(
        xp2, gm, wbig, wsm, wvt, cos_p, sa_p, sb_p, tm=512, seq=seq, prompt=True)
    wit_p = jnp.swapaxes(kiwi_p[:, IDX_DIM:IDX_DIM + IDX_HEADS].reshape(nbp, seq, IDX_HEADS), 1, 2)
    sh3 = lambda t, w: t.reshape(nbp, seq, w)
    attn_p = _attn_prompt_call(sh3(qi_p, a), wit_p, sh3(ki2_p, LANES), sh3(q_p, a), sh3(kb_p, a), vt_p, tq=256)
    yb_p, lconv_p, lh_p = _lru_prompt_call(x_prompt, gm, wxl, wgl, *lru_args, tc=512)
    y_p, fconv_p = _ffn_call(xp2, attn_p.reshape(nbp * seq, a), yb_p.reshape(nbp * seq, LRU_WIDTH), *ffn_w,
                             jnp.zeros((8, LANES), F32), tm=512, fc=1024, seq=seq, sample_nb=0)

    ns = nbs * nts
    xs2 = jnp.swapaxes(x_sample, 0, 1).reshape(ns, d)
    pos_s = jnp.repeat(past + jnp.arange(nts, dtype=I32), nbs)
    cos_s, sa_s, sb_s = _rope_tables(pos_s)
    q_s, qi_s, kf_s, vf_s, kiwi_s = _qkv_call(xs2, gm, wbig, wsm, wvt, cos_s, sa_s, sb_s,
                                              tm=ns, seq=nts, prompt=False)
    bm = lambda t, w: jnp.swapaxes(t.reshape(nts, nbs, w), 0, 1)
    k_s4 = bm(kf_s, a).reshape(nbs, nts, N_HEADS, HEAD_DIM)
    v_s4 = bm(vf_s, a).reshape(nbs, nts, N_HEADS, HEAD_DIM)
    ki_s = bm(kiwi_s[:, :IDX_DIM], IDX_DIM)
    wi_s = bm(kiwi_s[:, IDX_DIM:IDX_DIM + IDX_HEADS], IDX_HEADS)
    qi_hq = jnp.swapaxes(bm(qi_s, a).reshape(nbs, nts, IDX_HEADS, IDX_DIM), 1, 2).reshape(
        nbs, IDX_HEADS * nts, IDX_DIM)
    wi_hq = jnp.broadcast_to(jnp.swapaxes(wi_s, 1, 2).reshape(nbs, IDX_HEADS * nts, 1),
                             (nbs, IDX_HEADS * nts, LANES))
    kinew = jnp.pad(ki_s.astype(BF16), ((0, 0), (0, LANES - nts), (0, 0)))
    ptcol = jnp.pad(page_table.astype(F32), ((0, 0), (0, N_CHUNK_ROWS - n_pages)))[..., None]
    idx_s, newsel_s = _select_call(page_table, qi_hq, wi_hq, kinew, ptcol, cache_kidx[0], nt=nts)
    n_phys = cache_k.shape[1]
    o_s = _gather_call(idx_s.reshape(ns, TOPK), newsel_s[:, :, :nts].reshape(ns, nts),
                       bm(q_s, a).reshape(ns, N_HEADS, HEAD_DIM), k_s4, v_s4,
                       cache_k[0].reshape(n_phys * PAGE_SIZE, N_HEADS, HEAD_DIM),
                       cache_v[0].reshape(n_phys * PAGE_SIZE, N_HEADS, HEAD_DIM), nt=nts)
    attn_s = jnp.swapaxes(o_s.reshape(nbs, nts, a), 0, 1).reshape(ns, a).astype(BF16)
    tmaj = lambda s: jnp.swapaxes(s, 0, 1).reshape(-1, s.shape[-1])
    yb_s, lconv_s, lh_s = _lru_sample_call(xs2, gm, wxl, wgl, *lru_args, tmaj(state_lru_conv[0]),
                                           state_lru_h[0], nb=nbs, nt=nts)
    y_s, fconv_s = _ffn_call(xs2, attn_s, yb_s, *ffn_w, tmaj(state_ffn_conv[0]),
                             tm=ns, fc=1024, seq=nts, sample_nb=nbs)
    bmaj = lambda t, r: jnp.swapaxes(t.reshape(r, nbs, t.shape[-1]), 0, 1)

    return (y_p.reshape(nbp, seq, d),
            bmaj(y_s, nts),
            kf_p.reshape(1, nbp, seq, N_HEADS, HEAD_DIM),
            vf_p.reshape(1, nbp, seq, N_HEADS, HEAD_DIM),
            kiwi_p[:, :IDX_DIM].reshape(1, nbp, seq, IDX_DIM),
            lconv_p[None],
            lh_p.reshape(1, nbp, LRU_WIDTH),
            fconv_p.reshape(nbp, -1, 8, d_ff)[:, -1, 8 - (FFN_CONV_W - 1):][None],
            k_s4[None],
            v_s4[None],
            ki_s[None],
            bmaj(lconv_s, LRU_CONV_W - 1)[None],
            lh_s[None],
            bmaj(fconv_s, FFN_CONV_W - 1)[None])
```

```python
import functools

import numpy as np
import jax
import jax.numpy as jnp
from jax import lax
from jax.experimental import pallas as pl
from jax.experimental.pallas import tpu as pltpu

F32 = jnp.float32
BF16 = jnp.bfloat16
I32 = jnp.int32

N_HEADS = 8
HEAD_DIM = 64
ATTN_WIDTH = N_HEADS * HEAD_DIM
IDX_HEADS = 8
IDX_DIM = 64
TOPK = 256
LRU_WIDTH = 512
LRU_BLOCKS = 8
LRU_CONV_W = 4
LRU_C = 8.0
FFN_CONV_W = 3
ROPE_THETA = 10000.0
EPS = 1e-6
PAGE_SIZE = 128

LANES = 128
INT_MIN = -2 ** 31
NEG_BIG = -1e30
VMEM_LIMIT = 56 * 1024 * 1024

NT_DIMS = (((1,), (1,)), ((), ()))


def _cparams(n_axes):
    return pltpu.CompilerParams(dimension_semantics=("arbitrary",) * n_axes,
                                vmem_limit_bytes=VMEM_LIMIT)


def _rms(x, g):
    r = lax.rsqrt(jnp.mean(x * x, axis=-1, keepdims=True) + EPS)
    return x * r * g


def _gelu(x):
    c = np.float32(np.sqrt(2.0 / np.pi))
    return x * (0.5 * (1.0 + jnp.tanh(c * (x + np.float32(0.044715) * (x * x * x)))))


def _sigmoid(x):
    return 1.0 / (1.0 + jnp.exp(-x))


def _softplus(z):
    return jnp.maximum(z, 0.0) + jnp.log(1.0 + jnp.exp(-jnp.abs(z)))


def _order_key(x):
    x = jnp.where(x == 0.0, 0.0, x)
    bits = pltpu.bitcast(x, I32)
    return bits ^ ((bits >> 31) & jnp.int32(0x7FFFFFFF))


def _qkv_kernel(x_ref, g_ref, wbig_ref, wsm_ref, wvt_ref, cos_ref, sa_ref, sb_ref, *out_refs, prompt):
    if prompt:
        q_ref, qi_ref, kf_ref, kb_ref, vf_ref, vt_ref, kiwi_ref, ki2_ref = out_refs
    else:
        q_ref, qi_ref, kf_ref, vf_ref, kiwi_ref = out_refs
    h = _rms(x_ref[...], g_ref[...]).astype(BF16)
    y = jnp.dot(h, wbig_ref[...], preferred_element_type=F32)
    ys = jnp.dot(h, wsm_ref[...], preferred_element_type=F32)
    cos, sa, sb = cos_ref[...], sa_ref[...], sb_ref[...]

    def rope(t):
        return t * cos + pltpu.roll(t, 96, 1) * sa + pltpu.roll(t, 32, 1) * sb

    for j in range(ATTN_WIDTH // LANES):
        sl = slice(LANES * j, LANES * (j + 1))
        qj = rope(y[:, LANES * j:LANES * (j + 1)]) * 0.125
        q_ref[:, sl] = qj.astype(q_ref.dtype)
        kj = rope(y[:, ATTN_WIDTH + LANES * j:ATTN_WIDTH + LANES * (j + 1)])
        kf_ref[:, sl] = kj
        if prompt:
            kb_ref[:, sl] = kj.astype(BF16)
        qij = rope(y[:, 3 * ATTN_WIDTH + LANES * j:3 * ATTN_WIDTH + LANES * (j + 1)]) * 0.125
        qi_ref[:, sl] = qij.astype(BF16)
    vf_ref[...] = y[:, 2 * ATTN_WIDTH:3 * ATTN_WIDTH]
    ysr = rope(ys)
    lane = lax.broadcasted_iota(I32, ys.shape, 1)
    kiwi_ref[...] = jnp.where(lane < IDX_DIM, ysr, ys * np.float32(IDX_HEADS ** -0.5))
    if prompt:
        ki2_ref[...] = jnp.where(lane < IDX_DIM, ysr, pltpu.roll(ysr, IDX_DIM, 1)).astype(BF16)
        vt_ref[0] = lax.dot_general(wvt_ref[...], h, NT_DIMS,
                                    preferred_element_type=F32).astype(BF16)


def _qkv_call(x2d, g, wbig, wsm, wvt, cos, sa, sb, *, tm, seq, prompt):
    n, d = x2d.shape
    nt = n // tm
    tps = seq // tm if prompt else 1
    nb = n // seq if prompt else 1
    tok = lambda w: pl.BlockSpec((tm, w), lambda i: (i, 0))
    const = lambda a: pl.BlockSpec(a.shape, lambda i: (0,) * a.ndim)
    tab = pl.BlockSpec((tm, LANES), lambda i: (i % tps, 0))
    if prompt:
        out_shape = (jax.ShapeDtypeStruct((n, ATTN_WIDTH), BF16),
                     jax.ShapeDtypeStruct((n, ATTN_WIDTH), BF16),
                     jax.ShapeDtypeStruct((n, ATTN_WIDTH), F32),
                     jax.ShapeDtypeStruct((n, ATTN_WIDTH), BF16),
                     jax.ShapeDtypeStruct((n, ATTN_WIDTH), F32),
                     jax.ShapeDtypeStruct((nb, ATTN_WIDTH, seq), BF16),
                     jax.ShapeDtypeStruct((n, LANES), F32),
                     jax.ShapeDtypeStruct((n, LANES), BF16))
        out_specs = (tok(ATTN_WIDTH), tok(ATTN_WIDTH), tok(ATTN_WIDTH), tok(ATTN_WIDTH), tok(ATTN_WIDTH),
                     pl.BlockSpec((1, ATTN_WIDTH, tm), lambda i: (i // tps, 0, i % tps)),
                     tok(LANES), tok(LANES))
    else:
        out_shape = (jax.ShapeDtypeStruct((n, ATTN_WIDTH), F32),
                     jax.ShapeDtypeStruct((n, ATTN_WIDTH), BF16),
                     jax.ShapeDtypeStruct((n, ATTN_WIDTH), F32),
                     jax.ShapeDtypeStruct((n, ATTN_WIDTH), F32),
                     jax.ShapeDtypeStruct((n, LANES), F32))
        out_specs = (tok(ATTN_WIDTH), tok(ATTN_WIDTH), tok(ATTN_WIDTH), tok(ATTN_WIDTH), tok(LANES))
    return pl.pallas_call(
        functools.partial(_qkv_kernel, prompt=prompt),
        grid=(nt,),
        in_specs=[tok(d), const(g), const(wbig), const(wsm), const(wvt), tab, tab, tab],
        out_specs=out_specs, out_shape=out_shape,
        compiler_params=_cparams(1),
        name="qkv_prompt" if prompt else "qkv_sample",
    )(x2d, g, wbig, wsm, wvt, cos, sa, sb)


def _lru_gates(xc, wa_ref, ba_ref, wx_ref, bx_ref, lam_ref):
    xcb = xc.astype(BF16)
    r = _sigmoid(jnp.dot(xcb, wa_ref[...], preferred_element_type=F32) + ba_ref[...])
    i = _sigmoid(jnp.dot(xcb, wx_ref[...], preferred_element_type=F32) + bx_ref[...])
    log_a = -LRU_C * r * _softplus(-lam_ref[...])
    a = jnp.exp(log_a)
    mult = jnp.sqrt(1.0 - jnp.exp(2.0 * log_a))
    return a, mult * (i * xc)


def _lru_prompt_kernel(x_ref, g_ref, wxl_ref, wgl_ref, cw_ref, cb_ref, wa_ref, ba_ref, wx_ref, bx_ref,
                       lam_ref, y_ref, conv_ref, hlast_ref, xs_ref, a_ref, b_ref, hs_ref, hc_ref, *, tc):
    t = pl.program_id(1)

    @pl.when(t == 0)
    def _():
        xs_ref[0:8, :] = jnp.zeros((8, LRU_WIDTH), F32)
        hc_ref[...] = jnp.zeros(hc_ref.shape, F32)

    h = _rms(x_ref[0], g_ref[...]).astype(BF16)
    xl = jnp.dot(h, wxl_ref[...], preferred_element_type=F32)
    gl = jnp.dot(h, wgl_ref[...], preferred_element_type=F32)
    xs_ref[8:8 + tc, :] = xl
    cw = cw_ref[...]
    xc = cb_ref[...] + xs_ref[5:5 + tc, :] * cw[0:1]
    xc = xc + xs_ref[6:6 + tc, :] * cw[1:2]
    xc = xc + xs_ref[7:7 + tc, :] * cw[2:3]
    xc = xc + xl * cw[3:4]
    tail = xl[tc - 8:tc, :]
    xs_ref[0:8, :] = tail
    conv_ref[0] = tail[8 - (LRU_CONV_W - 1):8, :]

    a, bt = _lru_gates(xc, wa_ref, ba_ref, wx_ref, bx_ref, lam_ref)
    a_ref[...] = a
    b_ref[...] = bt

    def step(i, hp):
        hn = a_ref[pl.ds(i, 1), :] * hp + b_ref[pl.ds(i, 1), :]
        hs_ref[pl.ds(i, 1), :] = hn
        return hn

    hl = lax.fori_loop(0, tc, step, hc_ref[...], unroll=8)
    hc_ref[...] = hl
    hlast_ref[0] = hl
    y_ref[0] = (hs_ref[...] * _gelu(gl)).astype(BF16)


def _lru_prompt_call(x3d, g, wxl, wgl, cw, cb, wa, ba, wx, bx, lam, *, tc):
    nb, seq, d = x3d.shape
    const = lambda a: pl.BlockSpec(a.shape, lambda b, t: (0,) * a.ndim)
    return pl.pallas_call(
        functools.partial(_lru_prompt_kernel, tc=tc),
        grid=(nb, seq // tc),
        in_specs=[pl.BlockSpec((1, tc, d), lambda b, t: (b, t, 0))] +
                 [const(a) for a in (g, wxl, wgl, cw, cb, wa, ba, wx, bx, lam)],
        out_specs=(pl.BlockSpec((1, tc, LRU_WIDTH), lambda b, t: (b, t, 0)),
                   pl.BlockSpec((1, LRU_CONV_W - 1, LRU_WIDTH), lambda b, t: (b, 0, 0)),
                   pl.BlockSpec((1, 1, LRU_WIDTH), lambda b, t: (b, 0, 0))),
        out_shape=(jax.ShapeDtypeStruct((nb, seq, LRU_WIDTH), BF16),
                   jax.ShapeDtypeStruct((nb, LRU_CONV_W - 1, LRU_WIDTH), F32),
                   jax.ShapeDtypeStruct((nb, 1, LRU_WIDTH), F32)),
        scratch_shapes=[pltpu.VMEM((tc + 8, LRU_WIDTH), F32), pltpu.VMEM((tc, LRU_WIDTH), F32),
                        pltpu.VMEM((tc, LRU_WIDTH), F32), pltpu.VMEM((tc, LRU_WIDTH), F32),
                        pltpu.VMEM((1, LRU_WIDTH), F32)],
        compiler_params=_cparams(2),
        name="lru_prompt",
    )(x3d, g, wxl, wgl, cw, cb, wa, ba, wx, bx, lam)


def _lru_sample_kernel(x_ref, g_ref, wxl_ref, wgl_ref, cw_ref, cb_ref, wa_ref, ba_ref, wx_ref, bx_ref,
                       lam_ref, buf_ref, h0_ref, y_ref, conv_ref, hlast_ref, *, nb, nt):
    h = _rms(x_ref[...], g_ref[...]).astype(BF16)
    xl = jnp.dot(h, wxl_ref[...], preferred_element_type=F32)
    gl = jnp.dot(h, wgl_ref[...], preferred_element_type=F32)
    xx = jnp.concatenate([buf_ref[...], xl], axis=0)
    cw = cw_ref[...]
    n = nb * nt
    xc = cb_ref[...] + xx[0:n] * cw[0:1]
    for j in range(1, LRU_CONV_W):
        xc = xc + xx[j * nb:j * nb + n] * cw[j:j + 1]
    conv_ref[...] = xx[n:n + (LRU_CONV_W - 1) * nb]
    a, bt = _lru_gates(xc, wa_ref, ba_ref, wx_ref, bx_ref, lam_ref)
    hp = h0_ref[...]
    hs = []
    for t in range(nt):
        hp = a[t * nb:(t + 1) * nb] * hp + bt[t * nb:(t + 1) * nb]
        hs.append(hp)
    hlast_ref[...] = hp
    y_ref[...] = (jnp.concatenate(hs, axis=0) * _gelu(gl)).astype(BF16)


def _lru_sample_call(x2d, g, wxl, wgl, cw, cb, wa, ba, wx, bx, lam, buf, h0, *, nb, nt):
    n = nb * nt
    return pl.pallas_call(
        functools.partial(_lru_sample_kernel, nb=nb, nt=nt),
        out_shape=(jax.ShapeDtypeStruct((n, LRU_WIDTH), BF16),
                   jax.ShapeDtypeStruct(((LRU_CONV_W - 1) * nb, LRU_WIDTH), F32),
                   jax.ShapeDtypeStruct((nb, LRU_WIDTH), F32)),
        compiler_params=pltpu.CompilerParams(vmem_limit_bytes=VMEM_LIMIT),
        name="lru_sample",
    )(x2d, g, wxl, wgl, cw, cb, wa, ba, wx, bx, lam, buf, h0)


def _count_rows(m):
    r, c = m.shape
    part = jnp.sum(jnp.where(m, 1.0, 0.0).reshape(r // 8, 8, c), axis=0)
    return jnp.sum(part, axis=0, keepdims=True)


def _attn_prompt_kernel(qi_ref, wit_ref, ki2_ref, q_ref, k_ref, vt_ref, o_ref, key_ref, bias_ref, *, tq, tk):
    qb = pl.program_id(1)
    nk = qb + 1
    half = lax.broadcasted_iota(I32, (tq, LANES), 1) // HEAD_DIM
    kpos0 = lax.broadcasted_iota(I32, (tk, tq), 0)
    qpos = qb * tq + lax.broadcasted_iota(I32, (tk, tq), 1)

    def masked_pair(ref, h):
        pair = ref[0, :, LANES * (h // 2):LANES * (h // 2 + 1)]
        return jnp.where(half == (h % 2), pair, jnp.zeros_like(pair))

    qim = [masked_pair(qi_ref, h) for h in range(IDX_HEADS)]
    wit = wit_ref[0]

    def score_chunk(c, carry):
        off = pl.multiple_of(c * tk, tk)
        kc = ki2_ref[0, pl.ds(off, tk), :]
        acc = jnp.zeros((tk, tq), F32)
        for h in range(IDX_HEADS):
            s = lax.dot_general(kc, qim[h], NT_DIMS, preferred_element_type=F32)
            acc = acc + jnp.maximum(s, 0.0) * wit[h:h + 1, :]
        key = jnp.where(kpos0 + off <= qpos, _order_key(acc), jnp.int32(INT_MIN))
        key_ref[pl.ds(off, tk), :] = key
        return carry

    lax.fori_loop(0, nk, score_chunk, 0)

    def count_ge(trial):
        def body(c, part):
            off = pl.multiple_of(c * tk, tk)
            m = key_ref[pl.ds(off, tk), :] >= trial
            return part + jnp.sum(jnp.where(m, 1.0, 0.0).reshape(tk // 8, 8, tq), axis=0)
        part = lax.fori_loop(0, nk, body, jnp.zeros((8, tq), F32))
        return jnp.sum(part, axis=0, keepdims=True)

    def bisect(i, t):
        trial = t + lax.shift_left(jnp.int32(1), jnp.int32(31) - i)
        return jnp.where(count_ge(trial) >= float(TOPK), trial, t)

    t = lax.fori_loop(0, 32, bisect, jnp.full((1, tq), INT_MIN, I32))
    t = jnp.maximum(t, jnp.int32(INT_MIN + 1))
    n_ge = count_ge(t)
    has_ties = jnp.max(n_ge) > float(TOPK)

    def write_bias_simple():
        def body(c, carry):
            off = pl.multiple_of(c * tk, tk)
            bias_ref[pl.ds(off, tk), :] = jnp.where(key_ref[pl.ds(off, tk), :] >= t, 0.0, NEG_BIG)
            return carry
        lax.fori_loop(0, nk, body, 0)

    def write_bias_ties():
        def count_gt(c, cnt):
            off = pl.multiple_of(c * tk, tk)
            return cnt + _count_rows(key_ref[pl.ds(off, tk), :] > t)
        need = float(TOPK) - lax.fori_loop(0, nk, count_gt, jnp.zeros((1, tq), F32))

        def count_ties_below(jt):
            def body(c, cnt):
                off = pl.multiple_of(c * tk, tk)
                kc = key_ref[pl.ds(off, tk), :]
                return cnt + _count_rows(jnp.where(kpos0 + off < jt, kc, jnp.int32(INT_MIN)) == t)
            return lax.fori_loop(0, nk, body, jnp.zeros((1, tq), F32))

        def bis(i, j):
            jt = j + lax.shift_left(jnp.int32(1), jnp.int32(15) - i)
            return jnp.where(count_ties_below(jt) <= need, jt, j)

        jlim = lax.fori_loop(0, 16, bis, jnp.zeros((1, tq), I32))

        def body(c, carry):
            off = pl.multiple_of(c * tk, tk)
            kc = key_ref[pl.ds(off, tk), :]
            keep = kc >= jnp.where(kpos0 + off < jlim, t, t + 1)
            bias_ref[pl.ds(off, tk), :] = jnp.where(keep, 0.0, NEG_BIG)
            return carry
        lax.fori_loop(0, nk, body, 0)

    lax.cond(has_ties, write_bias_ties, write_bias_simple)

    qm = [masked_pair(q_ref, h) for h in range(N_HEADS)]

    def chunk(c, carry):
        off = pl.multiple_of(c * tk, tk)
        bias = bias_ref[pl.ds(off, tk), :]
        ss = []
        for h in range(N_HEADS):
            kc = k_ref[0, pl.ds(off, tk), LANES * (h // 2):LANES * (h // 2 + 1)]
            ss.append(lax.dot_general(kc, qm[h], NT_DIMS, preferred_element_type=F32))
        ps, stats = [], []
        for h in range(N_HEADS):
            m, l = carry[3 * h:3 * h + 2]
            s = ss[h] + bias
            m_new = jnp.maximum(m, jnp.max(s, axis=0, keepdims=True))
            p = jnp.exp(s - m_new)
            alpha = jnp.exp(m - m_new)
            stats.append((m_new, alpha * l + jnp.sum(p, axis=0, keepdims=True), alpha))
            ps.append(p.astype(BF16))
        new = []
        for h in range(N_HEADS):
            vt = vt_ref[0, HEAD_DIM * h:HEAD_DIM * (h + 1), pl.ds(off, tk)]
            m_new, l_new, alpha = stats[h]
            acc_new = alpha * carry[3 * h + 2] + jnp.dot(vt, ps[h], preferred_element_type=F32)
            new += [m_new, l_new, acc_new]
        return tuple(new)

    init = (jnp.full((1, tq), NEG_BIG, F32), jnp.zeros((1, tq), F32),
            jnp.zeros((HEAD_DIM, tq), F32)) * N_HEADS
    res = lax.fori_loop(0, nk, chunk, init)
    outs = [res[3 * h + 2] / res[3 * h + 1] for h in range(N_HEADS)]
    o_t = jnp.concatenate(outs, axis=0).astype(BF16)
    eye = (lax.broadcasted_iota(I32, (tq, tq), 0) == lax.broadcasted_iota(I32, (tq, tq), 1))
    eye = jnp.where(eye, 1.0, 0.0).astype(BF16)
    o_ref[0] = lax.dot_general(eye, o_t, NT_DIMS, preferred_element_type=F32).astype(BF16)


def _attn_prompt_call(qi, wit, ki2, q, k, vt, *, tq):
    nb, seq, _ = q.shape
    return pl.pallas_call(
        functools.partial(_attn_prompt_kernel, tq=tq, tk=tq),
        grid=(nb, seq // tq),
        in_specs=[pl.BlockSpec((1, tq, ATTN_WIDTH), lambda b, i: (b, i, 0)),
                  pl.BlockSpec((1, IDX_HEADS, tq), lambda b, i: (b, 0, i)),
                  pl.BlockSpec((1, seq, LANES), lambda b, i: (b, 0, 0)),
                  pl.BlockSpec((1, tq, ATTN_WIDTH), lambda b, i: (b, i, 0)),
                  pl.BlockSpec((1, seq, ATTN_WIDTH), lambda b, i: (b, 0, 0)),
                  pl.BlockSpec((1, ATTN_WIDTH, seq), lambda b, i: (b, 0, 0))],
        out_specs=pl.BlockSpec((1, tq, ATTN_WIDTH), lambda b, i: (b, i, 0)),
        out_shape=jax.ShapeDtypeStruct((nb, seq, ATTN_WIDTH), BF16),
        scratch_shapes=[pltpu.VMEM((seq, tq), I32), pltpu.VMEM((seq, tq), F32)],
        compiler_params=_cparams(2),
        name="attn_prompt",
    )(qi, wit, ki2, q, k, vt)


N_CHUNK_ROWS = 256


def _select_kernel(pt_ref, qi_ref, wi_ref, kinew_ref, ptcol_ref, kidx_hbm, idx_ref, newsel_ref,
                   kbuf, sem, sc_ref, *, n_pages, nt, group):
    b = pl.program_id(0)
    nb = pl.num_programs(0)
    slot = b % 2
    n_keys = n_pages * PAGE_SIZE

    def page_copy(bb, sl, p):
        return pltpu.make_async_copy(kidx_hbm.at[0, pt_ref[bb, p]],
                                     kbuf.at[sl, pl.ds(pl.multiple_of(p * PAGE_SIZE, PAGE_SIZE), PAGE_SIZE)],
                                     sem.at[sl])

    def fetch(bb, sl):
        def body(p, carry):
            page_copy(bb, sl, p).start()
            return carry
        lax.fori_loop(0, n_pages, body, 0)

    @pl.when(b == 0)
    def _():
        fetch(0, 0)

    @pl.when(b + 1 < nb)
    def _():
        fetch(b + 1, 1 - slot)

    def wait_body(p, carry):
        page_copy(b, slot, p).wait()
        return carry
    lax.fori_loop(0, n_pages, wait_body, 0)

    qi = qi_ref[0]
    wi = wi_ref[0]
    n_rows = IDX_HEADS * nt

    def head_sum(s):
        e = s[0:8]
        for r in range(1, n_rows // 8):
            e = e + s[8 * r:8 * r + 8]
        return e[0:nt] + e[nt:2 * nt]

    wi_g = jnp.concatenate([wi] * (group // LANES), axis=1)

    def score_group(gi, carry):
        off = pl.multiple_of(gi * group, group)
        kc = kbuf[slot, pl.ds(off, group), :].astype(BF16)
        s = lax.dot_general(qi, kc, NT_DIMS, preferred_element_type=F32)
        key = _order_key(head_sum(jnp.maximum(s, 0.0) * wi_g))
        row0 = gi * (group // LANES)
        for q in range(nt):
            for j in range(group // LANES):
                sc_ref[q, pl.ds(row0 + j, 1), :] = key[q:q + 1, LANES * j:LANES * (j + 1)]
        return carry

    lax.fori_loop(0, n_keys // group, score_group, 0)

    s_new = lax.dot_general(qi, kinew_ref[0], NT_DIMS, preferred_element_type=F32)
    key_new = _order_key(head_sum(jnp.maximum(s_new, 0.0) * wi))
    lane = lax.broadcasted_iota(I32, (nt, LANES), 1)
    qrow = lax.broadcasted_iota(I32, (nt, LANES), 0)
    key_new = jnp.where(lane <= qrow, key_new, jnp.int32(INT_MIN))
    for q in range(nt):
        sc_ref[q, n_pages:n_pages + 1, :] = key_new[q:q + 1, :]
        sc_ref[q, n_pages + 1:n_pages + 8, :] = jnp.full((7, LANES), INT_MIN, I32)
        sc_ref[q, n_pages + 8:N_CHUNK_ROWS, :] = jnp.full((N_CHUNK_ROWS - n_pages - 8, LANES), INT_MIN, I32)

    def count3(m):
        part = jnp.sum(jnp.where(m, 1.0, 0.0), axis=1, keepdims=True)
        return jnp.sum(part, axis=2, keepdims=True)

    def bisect(i, t):
        trial = t + lax.shift_left(jnp.int32(1), jnp.int32(31) - i)
        return jnp.where(count3(sc_ref[...] >= trial) >= float(TOPK), trial, t)

    t = lax.fori_loop(0, 32, bisect, jnp.full((nt, 1, 1), INT_MIN, I32))
    keys = sc_ref[...]
    need = float(TOPK) - count3(keys > t)
    pos = (lax.broadcasted_iota(I32, keys.shape, 1) * LANES + lax.broadcasted_iota(I32, keys.shape, 2))

    def bis(i, j):
        jt = j + lax.shift_left(jnp.int32(1), jnp.int32(16) - i)
        below = jnp.where(pos < jt, sc_ref[...], jnp.int32(INT_MIN)) == t
        return jnp.where(count3(below) <= need, jt, j)

    n_ge = count3(keys >= t)
    jlim = lax.cond(jnp.max(n_ge) > float(TOPK),
                    lambda: lax.fori_loop(0, 17, bis, jnp.zeros((nt, 1, 1), I32)),
                    lambda: jnp.full((nt, 1, 1), 2 ** 17 - 1, I32))
    sel = keys >= jnp.where(pos < jlim, t, t + 1)

    ri = lax.broadcasted_iota(I32, (N_CHUNK_ROWS, N_CHUNK_ROWS), 0)
    ci = lax.broadcasted_iota(I32, (N_CHUNK_ROWS, N_CHUNK_ROWS), 1)
    ltri_c = jnp.where(ci <= ri, 1.0, 0.0).astype(BF16)
    jrow = ci.astype(F32)
    ltri_k = ltri_c[0:LANES, 0:LANES]
    ones_k = jnp.ones((LANES, LANES), BF16)
    ptcol = ptcol_ref[0]
    for q in range(nt):
        m_bf = jnp.where(sel[q], 1.0, 0.0).astype(BF16)
        cnt = jnp.dot(m_bf, ones_k, preferred_element_type=F32)
        cum = jnp.dot(ltri_c, cnt.astype(BF16), preferred_element_type=F32)
        cum2 = jnp.concatenate([cum, cum], axis=1)
        cumx2 = cum2 - jnp.concatenate([cnt, cnt], axis=1)
        oh = jnp.where(cumx2 <= jrow, jnp.where(jrow < cum2, 1.0, 0.0), 0.0)
        phys = jnp.sum(oh * ptcol, axis=0, keepdims=True)
        cumx = jnp.sum(oh * cumx2, axis=0, keepdims=True)
        pc_t = lax.dot_general(ltri_k, m_bf, NT_DIMS, preferred_element_type=F32)
        pcg = jnp.dot(pc_t.astype(BF16), oh.astype(BF16),
                      preferred_element_type=F32)
        rank = jrow[0:1, :] - cumx
        slot_j = jnp.sum(jnp.where(pcg <= rank, 1.0, 0.0), axis=0, keepdims=True)
        idx_ref[0, q:q + 1, :] = (phys * float(PAGE_SIZE) + slot_j).astype(I32)
        newsel_ref[0, q:q + 1, :] = jnp.where(sel[q, n_pages:n_pages + 1, :], 1, 0).astype(I32)


def _select_call(page_table, qi_s, wi_s, kinew, ptcol, cache_kidx, *, nt):
    nb, n_pages = page_table.shape
    group = 1024
    grid_spec = pltpu.PrefetchScalarGridSpec(
        num_scalar_prefetch=1,
        grid=(nb,),
        in_specs=[pl.BlockSpec((1, IDX_HEADS * nt, IDX_DIM), lambda b, pt: (b, 0, 0)),
                  pl.BlockSpec((1, IDX_HEADS * nt, LANES), lambda b, pt: (b, 0, 0)),
                  pl.BlockSpec((1, LANES, IDX_DIM), lambda b, pt: (b, 0, 0)),
                  pl.BlockSpec((1, N_CHUNK_ROWS, 1), lambda b, pt: (b, 0, 0)),
                  pl.BlockSpec(memory_space=pl.ANY)],
        out_specs=(pl.BlockSpec((1, nt, TOPK), lambda b, pt: (b, 0, 0)),
                   pl.BlockSpec((1, nt, LANES), lambda b, pt: (b, 0, 0))),
        scratch_shapes=[pltpu.VMEM((2, n_pages * PAGE_SIZE, IDX_DIM), F32),
                        pltpu.SemaphoreType.DMA((2,)),
                        pltpu.VMEM((nt, N_CHUNK_ROWS, LANES), I32)])
    return pl.pallas_call(
        functools.partial(_select_kernel, n_pages=n_pages, nt=nt, group=group),
        grid_spec=grid_spec,
        out_shape=(jax.ShapeDtypeStruct((nb, nt, TOPK), I32),
                   jax.ShapeDtypeStruct((nb, nt, LANES), I32)),
        compiler_params=_cparams(1),
        name="select_sample",
    )(page_table, qi_s, wi_s, kinew, ptcol, cache_kidx)


def _gather_kernel(idx_ref, ns_ref, q_ref, knew_ref, vnew_ref, ck_hbm, cv_hbm, o_ref, kb, vb, sem, *, nt):
    n = pl.program_id(0)
    nn = pl.num_programs(0)
    slot = n % 2

    def fetch(row, sl):
        def body(j, carry):
            i = idx_ref[row, j]
            page = lax.shift_right_logical(i, PAGE_SIZE.bit_length() - 1)
            off = i & (PAGE_SIZE - 1)
            pltpu.make_async_copy(ck_hbm.at[0, page, off], kb.at[sl, j], sem.at[0, sl]).start()
            pltpu.make_async_copy(cv_hbm.at[0, page, off], vb.at[sl, j], sem.at[1, sl]).start()
            return carry
        lax.fori_loop(0, TOPK, body, 0, unroll=8)

    @pl.when(n == 0)
    def _():
        fetch(0, 0)

    @pl.when(n + 1 < nn)
    def _():
        fetch(n + 1, 1 - slot)

    for r in range(TOPK // PAGE_SIZE):
        rows = pl.ds(r * PAGE_SIZE, PAGE_SIZE)
        pltpu.make_async_copy(ck_hbm.at[0, 0], kb.at[slot, rows], sem.at[0, slot]).wait()
        pltpu.make_async_copy(cv_hbm.at[0, 0], vb.at[slot, rows], sem.at[1, slot]).wait()

    q = q_ref[0]
    n_new = ns_ref[n, 0]
    for j in range(1, nt):
        n_new = n_new + ns_ref[n, j]
    n_valid = TOPK - n_new
    s = jnp.sum(kb[slot] * q[None], axis=-1, keepdims=True)
    jio = lax.broadcasted_iota(I32, s.shape, 0)
    s = jnp.where(jio < n_valid, s, NEG_BIG)
    sn = jnp.sum(knew_ref[0] * q[None], axis=-1, keepdims=True)
    nio = lax.broadcasted_iota(I32, sn.shape, 0)
    seln = jnp.zeros(sn.shape, I32)
    for j in range(nt):
        seln = jnp.where(nio == j, ns_ref[n, j], seln)
    sn = jnp.where(seln > 0, sn, NEG_BIG)
    m = jnp.maximum(jnp.max(s, axis=0, keepdims=True), jnp.max(sn, axis=0, keepdims=True))
    p = jnp.exp(s - m)
    pn = jnp.exp(sn - m)
    l = jnp.sum(p, axis=0) + jnp.sum(pn, axis=0)
    o = jnp.sum(p * vb[slot], axis=0) + jnp.sum(pn * vnew_ref[0], axis=0)
    o_ref[0] = o / l


def _gather_call(idx2d, newsel2d, q3, knew, vnew, ck_rows, cv_rows, *, nt):
    n = idx2d.shape[0]
    grid_spec = pltpu.PrefetchScalarGridSpec(
        num_scalar_prefetch=2,
        grid=(n,),
        in_specs=[pl.BlockSpec((1, N_HEADS, HEAD_DIM), lambda i, a, b: (i, 0, 0)),
                  pl.BlockSpec((1, nt, N_HEADS, HEAD_DIM), lambda i, a, b: (i // nt, 0, 0, 0)),
                  pl.BlockSpec((1, nt, N_HEADS, HEAD_DIM), lambda i, a, b: (i // nt, 0, 0, 0)),
                  pl.BlockSpec(memory_space=pl.ANY),
                  pl.BlockSpec(memory_space=pl.ANY)],
        out_specs=pl.BlockSpec((1, N_HEADS, HEAD_DIM), lambda i, a, b: (i, 0, 0)),
        scratch_shapes=[pltpu.VMEM((2, TOPK, N_HEADS, HEAD_DIM), F32),
                        pltpu.VMEM((2, TOPK, N_HEADS, HEAD_DIM), F32),
                        pltpu.SemaphoreType.DMA((2, 2))])
    return pl.pallas_call(
        functools.partial(_gather_kernel, nt=nt),
        grid_spec=grid_spec,
        out_shape=jax.ShapeDtypeStruct((n, N_HEADS, HEAD_DIM), F32),
        compiler_params=_cparams(1),
        name="gather_sample",
    )(idx2d, newsel2d, q3, knew, vnew, ck_rows, cv_rows)


def _ffn_kernel(x_ref, at_ref, yb_ref, gm_ref, wga_ref, wgb_ref, wpa_ref, wpb_ref, wo_ref, gf_ref,
                wua_ref, wub_ref, fcw_ref, fcb_ref, wd_ref, gfin_ref, buf_ref,
                y_ref, st_ref, x1_ref, h2_ref, acc_ref, us_ref, carry_ref, *, tm, fc, tps, sample_nb):
    i = pl.program_id(0)
    c = pl.program_id(1)
    nc = pl.num_programs(1)

    @pl.when(c == 0)
    def _():
        x = x_ref[...]
        h = _rms(x, gm_ref[...]).astype(BF16)
        ga = jnp.dot(h, wga_ref[...], preferred_element_type=F32)
        gb = jnp.dot(h, wgb_ref[...], preferred_element_type=F32)
        ya = jnp.dot(at_ref[...], wpa_ref[...], preferred_element_type=F32)
        yb = jnp.dot(yb_ref[...], wpb_ref[...], preferred_element_type=F32)
        mix = _sigmoid(ga) * ya + _sigmoid(gb) * yb
        x1 = x + jnp.dot(mix.astype(BF16), wo_ref[...], preferred_element_type=F32)
        x1_ref[...] = x1
        h2_ref[...] = _rms(x1, gf_ref[...]).astype(BF16)
        acc_ref[...] = jnp.zeros(acc_ref.shape, F32)

    h2 = h2_ref[...]
    ua = jnp.dot(h2, wua_ref[...], preferred_element_type=F32)
    ub = jnp.dot(h2, wub_ref[...], preferred_element_type=F32)
    w = fcw_ref[...]
    if sample_nb:
        nb = sample_nb
        us = jnp.concatenate([buf_ref[...], ua], axis=0)
        uc = fcb_ref[...] + us[0:tm] * w[0:1]
        for j in range(1, FFN_CONV_W):
            uc = uc + us[j * nb:j * nb + tm] * w[j:j + 1]
        st_ref[...] = us[tm:tm + (FFN_CONV_W - 1) * nb]
    else:
        first = (i % tps) == 0
        us_ref[0:8, :] = jnp.where(first, jnp.zeros((8, fc), F32), carry_ref[c])
        us_ref[8:8 + tm, :] = ua
        uc = fcb_ref[...] + us_ref[6:6 + tm, :] * w[0:1]
        uc = uc + us_ref[7:7 + tm, :] * w[1:2]
        uc = uc + ua * w[2:3]
        tail = ua[tm - 8:tm, :]
        carry_ref[c] = tail
        st_ref[0] = tail
    act = (_gelu(uc) * ub).astype(BF16)
    acc_ref[...] += jnp.dot(act, wd_ref[...], preferred_element_type=F32)

    @pl.when(c == nc - 1)
    def _():
        y_ref[...] = _rms(x1_ref[...] + acc_ref[...], gfin_ref[...])


def _ffn_call(x2d, attn, yb, gm, wga, wgb, wpa, wpb, wo, gf, wup, fcw, fcb, wd, gfin, buf, *,
              tm, fc, seq, sample_nb):
    n, d = x2d.shape
    d_ff = wd.shape[0]
    nc = d_ff // fc
    tps = seq // tm if not sample_nb else 1
    tok = lambda w: pl.BlockSpec((tm, w), lambda i, c: (i, 0))
    const = lambda a: pl.BlockSpec(a.shape, lambda i, c: (0,) * a.ndim)
    if sample_nb:
        nst = (FFN_CONV_W - 1) * sample_nb
        buf_spec = pl.BlockSpec((nst, fc), lambda i, c: (0, c))
        st_spec = pl.BlockSpec((nst, fc), lambda i, c: (0, c))
        st_shape = jax.ShapeDtypeStruct((nst, d_ff), F32)
    else:
        buf_spec = pl.BlockSpec((8, LANES), lambda i, c: (0, 0))
        st_spec = pl.BlockSpec((1, 8, fc), lambda i, c: (i, 0, c))
        st_shape = jax.ShapeDtypeStruct((n // tm, 8, d_ff), F32)
    return pl.pallas_call(
        functools.partial(_ffn_kernel, tm=tm, fc=fc, tps=tps, sample_nb=sample_nb),
        grid=(n // tm, nc),
        in_specs=[tok(d), tok(ATTN_WIDTH), tok(LRU_WIDTH), const(gm), const(wga), const(wgb), const(wpa),
                  const(wpb), const(wo), const(gf),
                  pl.BlockSpec((d, fc), lambda i, c: (0, c)),
                  pl.BlockSpec((d, fc), lambda i, c: (0, nc + c)),
                  pl.BlockSpec((FFN_CONV_W, fc), lambda i, c: (0, c)),
                  pl.BlockSpec((1, fc), lambda i, c: (0, c)),
                  pl.BlockSpec((fc, d), lambda i, c: (c, 0)),
                  const(gfin), buf_spec],
        out_specs=(tok(d), st_spec),
        out_shape=(jax.ShapeDtypeStruct((n, d), F32), st_shape),
        scratch_shapes=[pltpu.VMEM((tm, d), F32), pltpu.VMEM((tm, d), BF16), pltpu.VMEM((tm, d), F32),
                        pltpu.VMEM((tm + 8, fc), F32), pltpu.VMEM((nc, 8, fc), F32)],
        compiler_params=_cparams(2),
        name="ffn_sample" if sample_nb else "ffn_prompt",
    )(x2d, attn, yb, gm, wga, wgb, wpa, wpb, wo, gf, wup, wup, fcw, fcb, wd, gfin, buf)


def _rope_tables(pos):
    half = HEAD_DIM // 2
    inv = jnp.power(ROPE_THETA, -jnp.arange(half, dtype=F32) / half)
    ang = pos.astype(F32)[:, None] * inv[None, :]
    cos, sin = jnp.cos(ang), jnp.sin(ang)
    z = jnp.zeros_like(sin)
    tile = lambda a, b: jnp.concatenate([a, b, a, b], axis=1)
    return tile(cos, cos), tile(-sin, z), tile(z, sin)


def _block_diag(w):
    nblk, bw, _ = w.shape
    eye = jnp.eye(nblk, dtype=w.dtype)
    return (eye[:, None, :, None] * w[:, :, None, :]).reshape(nblk * bw, nblk * bw)


def kernel(x_prompt, x_sample, cache_k, cache_v, cache_kidx, page_table, state_lru_conv, state_lru_h,
           state_ffn_conv, norm_mix_g, w_in, lru_conv_w, lru_conv_b, lru_wa, lru_ba, lru_wx, lru_bx,
           lru_lambda, w_proj_a, w_proj_b, w_out, norm_ffn_g, w_up, ffn_conv_w, ffn_conv_b, w_down,
           norm_final_g):
    depth = w_in.shape[0]
    assert depth == 1, "single-layer step"
    nbp, seq, d = x_prompt.shape
    nbs, nts, _ = x_sample.shape
    n_pages = page_table.shape[1]
    past = n_pages * PAGE_SIZE
    d_ff = w_down.shape[1]
    a = ATTN_WIDTH
    row = lambda v: v.reshape(1, -1)

    win = w_in[0]
    o_ki = 4 * a
    o_wi = o_ki + IDX_DIM
    o_xl = o_wi + IDX_HEADS
    o_gl = o_xl + LRU_WIDTH
    o_ga = o_gl + LRU_WIDTH
    o_gb = o_ga + d
    wbig = win[:, :4 * a].astype(BF16)
    wsm = jnp.pad(win[:, o_ki:o_xl], ((0, 0), (0, LANES - IDX_DIM - IDX_HEADS))).astype(BF16)
    wvt = win[:, 2 * a:3 * a].T.astype(BF16)
    wxl = win[:, o_xl:o_gl].astype(BF16)
    wgl = win[:, o_gl:o_ga].astype(BF16)
    wga = win[:, o_ga:o_gb].astype(BF16)
    wgb = win[:, o_gb:o_gb + d].astype(BF16)
    wa_bd = _block_diag(lru_wa[0]).astype(BF16)
    wx_bd = _block_diag(lru_wx[0]).astype(BF16)
    wpa = w_proj_a[0].astype(BF16)
    wpb = w_proj_b[0].astype(BF16)
    wo = w_out[0].astype(BF16)
    wup = w_up[0].astype(BF16)
    wd = w_down[0].astype(BF16)
    gm, gf, gfin = row(norm_mix_g[0]), row(norm_ffn_g[0]), row(norm_final_g)
    lru_args = (lru_conv_w[0], row(lru_conv_b[0]), wa_bd, row(lru_ba[0]), wx_bd, row(lru_bx[0]),
                row(lru_lambda[0]))
    ffn_w = (gm, wga, wgb, wpa, wpb, wo, gf, wup, ffn_conv_w[0], row(ffn_conv_b[0]), wd, gfin)

    xp2 = x_prompt.reshape(nbp * seq, d)
    cos_p, sa_p, sb_p = _rope_tables(jnp.arange(seq, dtype=I32))
    q_p, qi_p, kf_p, kb_p, vf_p, vt_p, kiwi_p, ki2_p = _qkv_call(
        xp2, gm, wbig, wsm, wvt, cos_p, sa_p, sb_p, tm=512, seq=seq, prompt=True)
    wit_p = jnp.swapaxes(kiwi_p[:, IDX_DIM:IDX_DIM + IDX_HEADS].reshape(nbp, seq, IDX_HEADS), 1, 2)
    sh3 = lambda t, w: t.reshape(nbp, seq, w)
    attn_p = _attn_prompt_call(sh3(qi_p, a), wit_p, sh3(ki2_p, LANES), sh3(q_p, a), sh3(kb_p, a), vt_p, tq=256)
    yb_p, lconv_p, lh_p = _lru_prompt_call(x_prompt, gm, wxl, wgl, *lru_args, tc=512)
    y_p, fconv_p = _ffn_call(xp2, attn_p.reshape(nbp * seq, a), yb_p.reshape(nbp * seq, LRU_WIDTH), *ffn_w,
                             jnp.zeros((8, LANES), F32), tm=512, fc=1024, seq=seq, sample_nb=0)

    ns = nbs * nts
    xs2 = jnp.swapaxes(x_sample, 0, 1).reshape(ns, d)
    pos_s = jnp.repeat(past + jnp.arange(nts, dtype=I32), nbs)
    cos_s, sa_s, sb_s = _rope_tables(pos_s)
    q_s, qi_s, kf_s, vf_s, kiwi_s = _qkv_call(xs2, gm, wbig, wsm, wvt, cos_s, sa_s, sb_s,
                                              tm=ns, seq=nts, prompt=False)
    bm = lambda t, w: jnp.swapaxes(t.reshape(nts, nbs, w), 0, 1)
    k_s4 = bm(kf_s, a).reshape(nbs, nts, N_HEADS, HEAD_DIM)
    v_s4 = bm(vf_s, a).reshape(nbs, nts, N_HEADS, HEAD_DIM)
    ki_s = bm(kiwi_s[:, :IDX_DIM], IDX_DIM)
    wi_s = bm(kiwi_s[:, IDX_DIM:IDX_DIM + IDX_HEADS], IDX_HEADS)
    qi_hq = jnp.swapaxes(bm(qi_s, a).reshape(nbs, nts, IDX_HEADS, IDX_DIM), 1, 2).reshape(
        nbs, IDX_HEADS * nts, IDX_DIM)
    wi_hq = jnp.broadcast_to(jnp.swapaxes(wi_s, 1, 2).reshape(nbs, IDX_HEADS * nts, 1),
                             (nbs, IDX_HEADS * nts, LANES))
    kinew = jnp.pad(ki_s.astype(BF16), ((0, 0), (0, LANES - nts), (0, 0)))
    ptcol = jnp.pad(page_table.astype(F32), ((0, 0), (0, N_CHUNK_ROWS - n_pages)))[..., None]
    idx_s, newsel_s = _select_call(page_table, qi_hq, wi_hq, kinew, ptcol, cache_kidx, nt=nts)
    o_s = _gather_call(idx_s.reshape(ns, TOPK), newsel_s[:, :, :nts].reshape(ns, nts),
                       bm(q_s, a).reshape(ns, N_HEADS, HEAD_DIM), k_s4, v_s4, cache_k, cache_v, nt=nts)
    attn_s = jnp.swapaxes(o_s.reshape(nbs, nts, a), 0, 1).reshape(ns, a).astype(BF16)
    tmaj = lambda s: jnp.swapaxes(s, 0, 1).reshape(-1, s.shape[-1])
    yb_s, lconv_s, lh_s = _lru_sample_call(xs2, gm, wxl, wgl, *lru_args, tmaj(state_lru_conv[0]),
                                           state_lru_h[0], nb=nbs, nt=nts)
    y_s, fconv_s = _ffn_call(xs2, attn_s, yb_s, *ffn_w, tmaj(state_ffn_conv[0]),
                             tm=ns, fc=1024, seq=nts, sample_nb=nbs)
    bmaj = lambda t, r: jnp.swapaxes(t.reshape(r, nbs, t.shape[-1]), 0, 1)

    return (y_p.reshape(nbp, seq, d),
            bmaj(y_s, nts),
            kf_p.reshape(1, nbp, seq, N_HEADS, HEAD_DIM),
            vf_p.reshape(1, nbp, seq, N_HEADS, HEAD_DIM),
            kiwi_p[:, :IDX_DIM].reshape(1, nbp, seq, IDX_DIM),
            lconv_p[None],
            lh_p.reshape(1, nbp, LRU_WIDTH),
            fconv_p.reshape(nbp, -1, 8, d_ff)[:, -1, 8 - (FFN_CONV_W - 1):][None],
            k_s4[None],
            v_s4[None],
            ki_s[None],
            bmaj(lconv_s, LRU_CONV_W - 1)[None],
            lh_s[None],
            bmaj(fconv_s, FFN_CONV_W - 1)[None])
```

```python
import functools

import numpy as np
import jax
import jax.numpy as jnp
from jax import lax
from jax.experimental import pallas as pl
from jax.experimental.pallas import tpu as pltpu

F32 = jnp.float32
BF16 = jnp.bfloat16
I32 = jnp.int32

N_HEADS = 8
HEAD_DIM = 64
ATTN_WIDTH = N_HEADS * HEAD_DIM
IDX_HEADS = 8
IDX_DIM = 64
TOPK = 256
LRU_WIDTH = 512
LRU_BLOCKS = 8
LRU_CONV_W = 4
LRU_C = 8.0
FFN_CONV_W = 3
ROPE_THETA = 10000.0
EPS = 1e-6
PAGE_SIZE = 128

LANES = 128
INT_MIN = -2 ** 31
NEG_BIG = -1e30
NEG_INF = float("-inf")
KEY_LOWEST = INT_MIN + 2 ** 23
VMEM_LIMIT = 56 * 1024 * 1024

NT_DIMS = (((1,), (1,)), ((), ()))


def _cparams(n_axes):
    return pltpu.CompilerParams(dimension_semantics=("arbitrary",) * n_axes,
                                vmem_limit_bytes=VMEM_LIMIT)


def _rms(x, g):
    r = lax.rsqrt(jnp.mean(x * x, axis=-1, keepdims=True) + EPS)
    return x * r * g


def _gelu(x):
    c = np.float32(np.sqrt(2.0 / np.pi))
    return x * (0.5 * (1.0 + jnp.tanh(c * (x + np.float32(0.044715) * (x * x * x)))))


def _sigmoid(x):
    return 1.0 / (1.0 + jnp.exp(-x))


def _softplus(z):
    return jnp.maximum(z, 0.0) + jnp.log(1.0 + jnp.exp(-jnp.abs(z)))


def _key_to_f32(k):
    return pltpu.bitcast(k ^ ((k >> 31) & jnp.int32(0x7FFFFFFF)), F32)


def _exact_threshold(count_ge, shape):
    def bisect(i, k):
        trial = k + lax.shift_left(jnp.int32(1), jnp.int32(31) - i)
        return jnp.where(count_ge(_key_to_f32(trial)) >= float(TOPK), trial, k)

    k = lax.fori_loop(0, 32, bisect, jnp.full(shape, INT_MIN, I32))
    k = jnp.maximum(k, jnp.int32(KEY_LOWEST))
    return _key_to_f32(k), _key_to_f32(k + 1)


def _qkv_kernel(x_ref, g_ref, wbig_ref, wsm_ref, wkvt_ref, cos_ref, sa_ref, sb_ref, cost_ref, sint_ref,
                *out_refs, prompt):
    if prompt:
        q_ref, qi_ref, kb_ref, ktf_ref, vtf_ref, vt_ref, kiwi_ref, ki2_ref = out_refs
    else:
        q_ref, qi_ref, kf_ref, vf_ref, kiwi_ref = out_refs
    h = _rms(x_ref[...], g_ref[...]).astype(BF16)
    y = jnp.dot(h, wbig_ref[...], preferred_element_type=F32)
    ys = jnp.dot(h, wsm_ref[...], preferred_element_type=F32)
    cos, sa, sb = cos_ref[...], sa_ref[...], sb_ref[...]
    o_qi = (2 if prompt else 3) * ATTN_WIDTH

    def rope(t):
        return t * cos + pltpu.roll(t, 96, 1) * sa + pltpu.roll(t, 32, 1) * sb

    for j in range(ATTN_WIDTH // LANES):
        sl = slice(LANES * j, LANES * (j + 1))
        qj = rope(y[:, LANES * j:LANES * (j + 1)]) * 0.125
        q_ref[:, sl] = qj.astype(q_ref.dtype)
        kj = rope(y[:, ATTN_WIDTH + LANES * j:ATTN_WIDTH + LANES * (j + 1)])
        if prompt:
            kb_ref[:, sl] = kj.astype(BF16)
        else:
            kf_ref[:, sl] = kj
        qij = rope(y[:, o_qi + LANES * j:o_qi + LANES * (j + 1)]) * 0.125
        qi_ref[:, sl] = qij.astype(BF16)
    ysr = rope(ys)
    lane = lax.broadcasted_iota(I32, ys.shape, 1)
    kiwi_ref[...] = jnp.where(lane < IDX_DIM, ysr, ys * np.float32(IDX_HEADS ** -0.5))
    if not prompt:
        vf_ref[...] = y[:, 2 * ATTN_WIDTH:3 * ATTN_WIDTH]
        return
    ki2_ref[...] = jnp.where(lane < IDX_DIM, ysr, pltpu.roll(ysr, IDX_DIM, 1)).astype(BF16)
    kvt = lax.dot_general(wkvt_ref[...], h, NT_DIMS, preferred_element_type=F32)
    cos_t, sin_t = cost_ref[...], sint_ref[...]
    hh = HEAD_DIM // 2
    for hd in range(N_HEADS):
        x1 = kvt[HEAD_DIM * hd:HEAD_DIM * hd + hh]
        x2 = kvt[HEAD_DIM * hd + hh:HEAD_DIM * (hd + 1)]
        ktf_ref[0, HEAD_DIM * hd:HEAD_DIM * hd + hh, :] = x1 * cos_t - x2 * sin_t
        ktf_ref[0, HEAD_DIM * hd + hh:HEAD_DIM * (hd + 1), :] = x2 * cos_t + x1 * sin_t
    v_t = kvt[ATTN_WIDTH:2 * ATTN_WIDTH]
    vtf_ref[0] = v_t
    vt_ref[0] = v_t.astype(BF16)


def _qkv_call(x2d, g, wbig, wsm, wkvt, cos, sa, sb, cos_t, sin_t, *, tm, seq, prompt):
    n, d = x2d.shape
    nt = n // tm
    tps = seq // tm if prompt else 1
    nb = n // seq if prompt else 1
    tok = lambda w: pl.BlockSpec((tm, w), lambda i: (i, 0))
    const = lambda a: pl.BlockSpec(a.shape, lambda i: (0,) * a.ndim)
    tab = pl.BlockSpec((tm, LANES), lambda i: (i % tps, 0))
    tab_t = pl.BlockSpec((HEAD_DIM // 2, tm), lambda i: (0, i % tps))
    if prompt:
        seq_t = pl.BlockSpec((1, ATTN_WIDTH, tm), lambda i: (i // tps, 0, i % tps))
        out_shape = (jax.ShapeDtypeStruct((n, ATTN_WIDTH), BF16),
                     jax.ShapeDtypeStruct((n, ATTN_WIDTH), BF16),
                     jax.ShapeDtypeStruct((n, ATTN_WIDTH), BF16),
                     jax.ShapeDtypeStruct((nb, ATTN_WIDTH, seq), F32),
                     jax.ShapeDtypeStruct((nb, ATTN_WIDTH, seq), F32),
                     jax.ShapeDtypeStruct((nb, ATTN_WIDTH, seq), BF16),
                     jax.ShapeDtypeStruct((n, LANES), F32),
                     jax.ShapeDtypeStruct((n, LANES), BF16))
        out_specs = (tok(ATTN_WIDTH), tok(ATTN_WIDTH), tok(ATTN_WIDTH), seq_t, seq_t, seq_t,
                     tok(LANES), tok(LANES))
    else:
        out_shape = (jax.ShapeDtypeStruct((n, ATTN_WIDTH), F32),
                     jax.ShapeDtypeStruct((n, ATTN_WIDTH), BF16),
                     jax.ShapeDtypeStruct((n, ATTN_WIDTH), F32),
                     jax.ShapeDtypeStruct((n, ATTN_WIDTH), F32),
                     jax.ShapeDtypeStruct((n, LANES), F32))
        out_specs = (tok(ATTN_WIDTH), tok(ATTN_WIDTH), tok(ATTN_WIDTH), tok(ATTN_WIDTH), tok(LANES))
    return pl.pallas_call(
        functools.partial(_qkv_kernel, prompt=prompt),
        grid=(nt,),
        in_specs=[tok(d), const(g), const(wbig), const(wsm), const(wkvt), tab, tab, tab, tab_t, tab_t],
        out_specs=out_specs, out_shape=out_shape,
        compiler_params=_cparams(1),
        name="qkv_prompt" if prompt else "qkv_sample",
    )(x2d, g, wbig, wsm, wkvt, cos, sa, sb, cos_t, sin_t)


def _lru_gates(xc, wa_ref, ba_ref, wx_ref, bx_ref, lam_ref):
    xcb = xc.astype(BF16)
    r = _sigmoid(jnp.dot(xcb, wa_ref[...], preferred_element_type=F32) + ba_ref[...])
    i = _sigmoid(jnp.dot(xcb, wx_ref[...], preferred_element_type=F32) + bx_ref[...])
    log_a = -LRU_C * r * _softplus(-lam_ref[...])
    a = jnp.exp(log_a)
    mult = jnp.sqrt(1.0 - jnp.exp(2.0 * log_a))
    return a, mult * (i * xc)


def _lru_prompt_kernel(x_ref, g_ref, wxl_ref, wgl_ref, cw_ref, cb_ref, wa_ref, ba_ref, wx_ref, bx_ref,
                       lam_ref, y_ref, conv_ref, hlast_ref, xs_ref, a_ref, b_ref, hs_ref, hc_ref, *, tc):
    t = pl.program_id(1)

    @pl.when(t == 0)
    def _():
        xs_ref[0:8, :] = jnp.zeros((8, LRU_WIDTH), F32)
        hc_ref[...] = jnp.zeros(hc_ref.shape, F32)

    h = _rms(x_ref[0], g_ref[...]).astype(BF16)
    xl = jnp.dot(h, wxl_ref[...], preferred_element_type=F32)
    gl = jnp.dot(h, wgl_ref[...], preferred_element_type=F32)
    xs_ref[8:8 + tc, :] = xl
    cw = cw_ref[...]
    xc = cb_ref[...] + xs_ref[5:5 + tc, :] * cw[0:1]
    xc = xc + xs_ref[6:6 + tc, :] * cw[1:2]
    xc = xc + xs_ref[7:7 + tc, :] * cw[2:3]
    xc = xc + xl * cw[3:4]
    tail = xl[tc - 8:tc, :]
    xs_ref[0:8, :] = tail
    conv_ref[0] = tail[8 - (LRU_CONV_W - 1):8, :]

    a, bt = _lru_gates(xc, wa_ref, ba_ref, wx_ref, bx_ref, lam_ref)
    a_ref[...] = a
    b_ref[...] = bt

    def step(i, hp):
        hn = a_ref[pl.ds(i, 1), :] * hp + b_ref[pl.ds(i, 1), :]
        hs_ref[pl.ds(i, 1), :] = hn
        return hn

    hl = lax.fori_loop(0, tc, step, hc_ref[...], unroll=8)
    hc_ref[...] = hl
    hlast_ref[0] = hl
    y_ref[0] = (hs_ref[...] * _gelu(gl)).astype(BF16)


def _lru_prompt_call(x3d, g, wxl, wgl, cw, cb, wa, ba, wx, bx, lam, *, tc):
    nb, seq, d = x3d.shape
    const = lambda a: pl.BlockSpec(a.shape, lambda b, t: (0,) * a.ndim)
    return pl.pallas_call(
        functools.partial(_lru_prompt_kernel, tc=tc),
        grid=(nb, seq // tc),
        in_specs=[pl.BlockSpec((1, tc, d), lambda b, t: (b, t, 0))] +
                 [const(a) for a in (g, wxl, wgl, cw, cb, wa, ba, wx, bx, lam)],
        out_specs=(pl.BlockSpec((1, tc, LRU_WIDTH), lambda b, t: (b, t, 0)),
                   pl.BlockSpec((1, LRU_CONV_W - 1, LRU_WIDTH), lambda b, t: (b, 0, 0)),
                   pl.BlockSpec((1, 1, LRU_WIDTH), lambda b, t: (b, 0, 0))),
        out_shape=(jax.ShapeDtypeStruct((nb, seq, LRU_WIDTH), BF16),
                   jax.ShapeDtypeStruct((nb, LRU_CONV_W - 1, LRU_WIDTH), F32),
                   jax.ShapeDtypeStruct((nb, 1, LRU_WIDTH), F32)),
        scratch_shapes=[pltpu.VMEM((tc + 8, LRU_WIDTH), F32), pltpu.VMEM((tc, LRU_WIDTH), F32),
                        pltpu.VMEM((tc, LRU_WIDTH), F32), pltpu.VMEM((tc, LRU_WIDTH), F32),
                        pltpu.VMEM((1, LRU_WIDTH), F32)],
        compiler_params=_cparams(2),
        name="lru_prompt",
    )(x3d, g, wxl, wgl, cw, cb, wa, ba, wx, bx, lam)


def _lru_sample_kernel(x_ref, g_ref, wxl_ref, wgl_ref, cw_ref, cb_ref, wa_ref, ba_ref, wx_ref, bx_ref,
                       lam_ref, buf_ref, h0_ref, y_ref, conv_ref, hlast_ref, *, nb, nt):
    h = _rms(x_ref[...], g_ref[...]).astype(BF16)
    xl = jnp.dot(h, wxl_ref[...], preferred_element_type=F32)
    gl = jnp.dot(h, wgl_ref[...], preferred_element_type=F32)
    xx = jnp.concatenate([buf_ref[...], xl], axis=0)
    cw = cw_ref[...]
    n = nb * nt
    xc = cb_ref[...] + xx[0:n] * cw[0:1]
    for j in range(1, LRU_CONV_W):
        xc = xc + xx[j * nb:j * nb + n] * cw[j:j + 1]
    conv_ref[...] = xx[n:n + (LRU_CONV_W - 1) * nb]
    a, bt = _lru_gates(xc, wa_ref, ba_ref, wx_ref, bx_ref, lam_ref)
    hp = h0_ref[...]
    hs = []
    for t in range(nt):
        hp = a[t * nb:(t + 1) * nb] * hp + bt[t * nb:(t + 1) * nb]
        hs.append(hp)
    hlast_ref[...] = hp
    y_ref[...] = (jnp.concatenate(hs, axis=0) * _gelu(gl)).astype(BF16)


def _lru_sample_call(x2d, g, wxl, wgl, cw, cb, wa, ba, wx, bx, lam, buf, h0, *, nb, nt):
    n = nb * nt
    return pl.pallas_call(
        functools.partial(_lru_sample_kernel, nb=nb, nt=nt),
        out_shape=(jax.ShapeDtypeStruct((n, LRU_WIDTH), BF16),
                   jax.ShapeDtypeStruct(((LRU_CONV_W - 1) * nb, LRU_WIDTH), F32),
                   jax.ShapeDtypeStruct((nb, LRU_WIDTH), F32)),
        compiler_params=pltpu.CompilerParams(vmem_limit_bytes=VMEM_LIMIT),
        name="lru_sample",
    )(x2d, g, wxl, wgl, cw, cb, wa, ba, wx, bx, lam, buf, h0)


def _attn_prompt_kernel(qi_ref, wit_ref, ki2_ref, q_ref, k_ref, vt_ref, o_ref, sc_ref, bias_ref, *, tq, tk):
    qb = pl.program_id(1)
    nk = qb + 1
    half = lax.broadcasted_iota(I32, (tq, LANES), 1) // HEAD_DIM
    kpos0 = lax.broadcasted_iota(I32, (tk, tq), 0)
    qpos = qb * tq + lax.broadcasted_iota(I32, (tk, tq), 1)

    def masked_pair(ref, h):
        pair = ref[0, :, LANES * (h // 2):LANES * (h // 2 + 1)]
        return jnp.where(half == (h % 2), pair, jnp.zeros_like(pair))

    qim = [masked_pair(qi_ref, h) for h in range(IDX_HEADS)]
    wit = wit_ref[0]

    def score_chunk(c, carry):
        off = pl.multiple_of(c * tk, tk)
        kc = ki2_ref[0, pl.ds(off, tk), :]
        acc = jnp.zeros((tk, tq), F32)
        for h in range(IDX_HEADS):
            s = lax.dot_general(kc, qim[h], NT_DIMS, preferred_element_type=F32)
            acc = acc + jnp.maximum(s, 0.0) * wit[h:h + 1, :]
        sc_ref[pl.ds(off, tk), :] = jnp.where(kpos0 + off <= qpos, acc, NEG_INF)
        return carry

    lax.fori_loop(0, nk, score_chunk, 0)

    def count_where(pred):
        def body(c, part):
            off = pl.multiple_of(c * tk, tk)
            m = pred(sc_ref[pl.ds(off, tk), :], off)
            return part + jnp.sum(jnp.where(m, 1.0, 0.0).reshape(tk // 8, 8, tq), axis=0)
        part = lax.fori_loop(0, nk, body, jnp.zeros((8, tq), F32))
        return jnp.sum(part, axis=0, keepdims=True)

    def count_ge(trial):
        return count_where(lambda sc, off: sc >= trial)

    t, t_next = _exact_threshold(count_ge, (1, tq))

    def tie_limit():
        need = float(TOPK) - count_ge(t_next)

        def ties_below(jt):
            def pred_ge(thr):
                return lambda sc, off: jnp.where(kpos0 + off < jt, sc, NEG_INF) >= thr
            return count_where(pred_ge(t)) - count_where(pred_ge(t_next))

        def bis(i, j):
            jt = j + lax.shift_left(jnp.int32(1), jnp.int32(15) - i)
            return jnp.where(ties_below(jt) <= need, jt, j)

        return lax.fori_loop(0, 16, bis, jnp.zeros((1, tq), I32))

    jlim = lax.cond(jnp.max(count_ge(t)) > float(TOPK), tie_limit,
                    lambda: jnp.full((1, tq), 2 ** 16 - 1, I32))

    def write_bias(c, carry):
        off = pl.multiple_of(c * tk, tk)
        thr = jnp.where(kpos0 + off < jlim, t, t_next)
        bias_ref[pl.ds(off, tk), :] = jnp.where(sc_ref[pl.ds(off, tk), :] >= thr, 0.0, NEG_BIG)
        return carry

    lax.fori_loop(0, nk, write_bias, 0)

    qm = [masked_pair(q_ref, h) for h in range(N_HEADS)]

    def chunk(c, carry):
        off = pl.multiple_of(c * tk, tk)
        bias = bias_ref[pl.ds(off, tk), :]
        ss = []
        for h in range(N_HEADS):
            kc = k_ref[0, pl.ds(off, tk), LANES * (h // 2):LANES * (h // 2 + 1)]
            ss.append(lax.dot_general(kc, qm[h], NT_DIMS, preferred_element_type=F32))
        ps, stats = [], []
        for h in range(N_HEADS):
            m, l = carry[3 * h:3 * h + 2]
            s = ss[h] + bias
            m_new = jnp.maximum(m, jnp.max(s, axis=0, keepdims=True))
            p = jnp.exp(s - m_new)
            alpha = jnp.exp(m - m_new)
            stats.append((m_new, alpha * l + jnp.sum(p, axis=0, keepdims=True), alpha))
            ps.append(p.astype(BF16))
        new = []
        for h in range(N_HEADS):
            vt = vt_ref[0, HEAD_DIM * h:HEAD_DIM * (h + 1), pl.ds(off, tk)]
            m_new, l_new, alpha = stats[h]
            acc_new = alpha * carry[3 * h + 2] + jnp.dot(vt, ps[h], preferred_element_type=F32)
            new += [m_new, l_new, acc_new]
        return tuple(new)

    init = (jnp.full((1, tq), NEG_BIG, F32), jnp.zeros((1, tq), F32),
            jnp.zeros((HEAD_DIM, tq), F32)) * N_HEADS
    res = lax.fori_loop(0, nk, chunk, init)
    outs = [res[3 * h + 2] / res[3 * h + 1] for h in range(N_HEADS)]
    o_t = jnp.concatenate(outs, axis=0).astype(BF16)
    eye = (lax.broadcasted_iota(I32, (tq, tq), 0) == lax.broadcasted_iota(I32, (tq, tq), 1))
    eye = jnp.where(eye, 1.0, 0.0).astype(BF16)
    o_ref[0] = lax.dot_general(eye, o_t, NT_DIMS, preferred_element_type=F32).astype(BF16)


def _attn_prompt_call(qi, wit, ki2, q, k, vt, *, tq):
    nb, seq, _ = q.shape
    return pl.pallas_call(
        functools.partial(_attn_prompt_kernel, tq=tq, tk=tq),
        grid=(nb, seq // tq),
        in_specs=[pl.BlockSpec((1, tq, ATTN_WIDTH), lambda b, i: (b, i, 0)),
                  pl.BlockSpec((1, IDX_HEADS, tq), lambda b, i: (b, 0, i)),
                  pl.BlockSpec((1, seq, LANES), lambda b, i: (b, 0, 0)),
                  pl.BlockSpec((1, tq, ATTN_WIDTH), lambda b, i: (b, i, 0)),
                  pl.BlockSpec((1, seq, ATTN_WIDTH), lambda b, i: (b, 0, 0)),
                  pl.BlockSpec((1, ATTN_WIDTH, seq), lambda b, i: (b, 0, 0))],
        out_specs=pl.BlockSpec((1, tq, ATTN_WIDTH), lambda b, i: (b, i, 0)),
        out_shape=jax.ShapeDtypeStruct((nb, seq, ATTN_WIDTH), BF16),
        scratch_shapes=[pltpu.VMEM((seq, tq), F32), pltpu.VMEM((seq, tq), F32)],
        compiler_params=_cparams(2),
        name="attn_prompt",
    )(qi, wit, ki2, q, k, vt)


def _chunk_rows(n_pages):
    return -(-(n_pages + 1) // 8) * 8


def _select_kernel(pt_ref, qi_ref, wi_ref, kinew_ref, kidx_hbm, bias_ref, kbuf, sem, sc_ref, *,
                   n_pages, nt, group):
    b = pl.program_id(0)
    nb = pl.num_programs(0)
    slot = b % 2
    n_rows_sc = _chunk_rows(n_pages)

    def page_copy(bb, sl, p):
        return pltpu.make_async_copy(kidx_hbm.at[0, pt_ref[bb, p]], kbuf.at[sl, p], sem.at[sl])

    def fetch(bb, sl):
        def body(p, carry):
            page_copy(bb, sl, p).start()
            return carry
        lax.fori_loop(0, n_pages, body, 0)

    @pl.when(b == 0)
    def _():
        fetch(0, 0)

    @pl.when(b + 1 < nb)
    def _():
        fetch(b + 1, 1 - slot)

    def wait_body(p, carry):
        page_copy(b, slot, p).wait()
        return carry
    lax.fori_loop(0, n_pages, wait_body, 0)

    qi = qi_ref[0]
    wi = wi_ref[0]
    n_rows = IDX_HEADS * nt

    def head_sum(s):
        e = s[0:8]
        for r in range(1, n_rows // 8):
            e = e + s[8 * r:8 * r + 8]
        return e[0:nt] + e[nt:2 * nt]

    wi_g = jnp.concatenate([wi] * group, axis=1)

    def score_group(gi, carry):
        row0 = gi * group
        kc = jnp.concatenate([kbuf[slot, row0 + j] for j in range(group)], axis=1).astype(BF16)
        s = jnp.dot(qi, kc, preferred_element_type=F32)
        sc = head_sum(jnp.maximum(s, 0.0) * wi_g)
        for q in range(nt):
            for j in range(group):
                sc_ref[q, pl.ds(row0 + j, 1), :] = sc[q:q + 1, LANES * j:LANES * (j + 1)]
        return carry

    lax.fori_loop(0, n_pages // group, score_group, 0)

    s_new = lax.dot_general(qi, kinew_ref[0], NT_DIMS, preferred_element_type=F32)
    sc_new = head_sum(jnp.maximum(s_new, 0.0) * wi)
    lane = lax.broadcasted_iota(I32, (nt, LANES), 1)
    qrow = lax.broadcasted_iota(I32, (nt, LANES), 0)
    sc_new = jnp.where(lane <= qrow, sc_new, NEG_INF)
    for q in range(nt):
        sc_ref[q, n_pages:n_pages + 1, :] = sc_new[q:q + 1, :]
        sc_ref[q, n_pages + 1:n_rows_sc, :] = jnp.full((n_rows_sc - n_pages - 1, LANES), NEG_INF, F32)

    def reduce3(x, op):
        return op(op(x, axis=1, keepdims=True), axis=2, keepdims=True)

    def count_ge(trial, limit=None):
        sc = sc_ref[...]
        if limit is not None:
            sc = jnp.where(pos < limit, sc, NEG_INF)
        return reduce3(jnp.where(sc >= trial, 1.0, 0.0), jnp.sum)

    scores = sc_ref[...]
    pos = (lax.broadcasted_iota(I32, scores.shape, 1) * LANES + lax.broadcasted_iota(I32, scores.shape, 2))
    t, t_next = _exact_threshold(count_ge, (nt, 1, 1))

    def tie_limit():
        need = float(TOPK) - count_ge(t_next)

        def bis(i, j):
            jt = j + lax.shift_left(jnp.int32(1), jnp.int32(16) - i)
            return jnp.where(count_ge(t, jt) - count_ge(t_next, jt) <= need, jt, j)

        return lax.fori_loop(0, 17, bis, jnp.zeros((nt, 1, 1), I32))

    jlim = lax.cond(jnp.max(count_ge(t)) > float(TOPK), tie_limit,
                    lambda: jnp.full((nt, 1, 1), 2 ** 17 - 1, I32))
    thr = jnp.where(pos < jlim, t, t_next)
    bias_ref[0] = jnp.where(scores >= thr, 0.0, NEG_BIG)


def _select_call(page_table, qi_s, wi_s, kinew, kidx_t, *, nt):
    nb, n_pages = page_table.shape
    rows = _chunk_rows(n_pages)
    grid_spec = pltpu.PrefetchScalarGridSpec(
        num_scalar_prefetch=1,
        grid=(nb,),
        in_specs=[pl.BlockSpec((1, IDX_HEADS * nt, IDX_DIM), lambda b, pt: (b, 0, 0)),
                  pl.BlockSpec((1, IDX_HEADS * nt, LANES), lambda b, pt: (b, 0, 0)),
                  pl.BlockSpec((1, LANES, IDX_DIM), lambda b, pt: (b, 0, 0)),
                  pl.BlockSpec(memory_space=pl.ANY)],
        out_specs=pl.BlockSpec((1, nt, rows, LANES), lambda b, pt: (b, 0, 0, 0)),
        scratch_shapes=[pltpu.VMEM((2, n_pages, IDX_DIM, PAGE_SIZE), F32),
                        pltpu.SemaphoreType.DMA((2,)),
                        pltpu.VMEM((nt, rows, LANES), F32)])
    return pl.pallas_call(
        functools.partial(_select_kernel, n_pages=n_pages, nt=nt, group=8),
        grid_spec=grid_spec,
        out_shape=jax.ShapeDtypeStruct((nb, nt, rows, LANES), F32),
        compiler_params=_cparams(1),
        name="select_sample",
    )(page_table, qi_s, wi_s, kinew, kidx_t)


Q_ROWS = 16
S_ROWS = 8


def _dense_sample_kernel(pt_ref, q_ref, bias_ref, biasn_ref, knew_ref, vnew_ref, ck_hbm, cv_hbm, o_ref,
                         kb, vb, sem, m_ref, l_ref, acc_ref, *, gp):
    b = pl.program_id(0)
    g = pl.program_id(1)
    ng = pl.num_programs(1)
    n_steps = pl.num_programs(0) * ng
    step = b * ng + g
    slot = step % 2

    def page_copies(st, sl, j):
        page = pt_ref[lax.div(st, ng), lax.rem(st, ng) * gp + j]
        return (pltpu.make_async_copy(ck_hbm.at[0, page], kb.at[sl, j], sem.at[0, sl]),
                pltpu.make_async_copy(cv_hbm.at[0, page], vb.at[sl, j], sem.at[1, sl]))

    def fetch(st, sl):
        for j in range(gp):
            for cp in page_copies(st, sl, j):
                cp.start()

    @pl.when(step == 0)
    def _():
        fetch(0, 0)

    @pl.when(step + 1 < n_steps)
    def _():
        fetch(step + 1, 1 - slot)

    for j in range(gp):
        for cp in page_copies(step, slot, j):
            cp.wait()

    @pl.when(g == 0)
    def _():
        m_ref[...] = jnp.full(m_ref.shape, NEG_BIG, F32)
        l_ref[...] = jnp.zeros(l_ref.shape, F32)
        acc_ref[...] = jnp.zeros(acc_ref.shape, F32)

    def attend(k_of, v_of, bias):
        ss = [jnp.dot(q_ref[0, h], k_of(h), preferred_element_type=F32)[0:S_ROWS] for h in range(N_HEADS)]
        ps, alphas = [], []
        for h in range(N_HEADS):
            s = ss[h] + bias
            m_old = m_ref[h]
            m_new = jnp.maximum(m_old, jnp.max(s, axis=1, keepdims=True))
            p = jnp.exp(s - m_new)
            alpha = jnp.exp(m_old - m_new)
            l_ref[h] = alpha * l_ref[h] + jnp.sum(p, axis=1, keepdims=True)
            m_ref[h] = m_new
            ps.append(jnp.concatenate([p, jnp.zeros((Q_ROWS - S_ROWS, p.shape[1]), F32)], axis=0).astype(BF16))
            alphas.append(alpha)
        for h in range(N_HEADS):
            pv = lax.dot_general(ps[h], v_of(h), NT_DIMS, preferred_element_type=F32)[0:S_ROWS]
            acc_ref[h] = alphas[h] * acc_ref[h] + pv

    def cat(buf, h):
        return jnp.concatenate([buf[slot, j, h] for j in range(gp)], axis=1).astype(BF16)

    attend(lambda h: cat(kb, h), lambda h: cat(vb, h),
           jnp.concatenate([bias_ref[0, j] for j in range(gp)], axis=1))

    @pl.when(g == ng - 1)
    def _():
        attend(lambda h: knew_ref[0, h].astype(BF16), lambda h: vnew_ref[0, h].astype(BF16), biasn_ref[0, 0])
        for h in range(N_HEADS):
            o_ref[0, h] = acc_ref[h] / l_ref[h]


def _dense_sample_call(page_table, qh, bias_t, knew_t, vnew_t, ck_t, cv_t, *, gp):
    nb, n_pages = page_table.shape
    grid_spec = pltpu.PrefetchScalarGridSpec(
        num_scalar_prefetch=1,
        grid=(nb, n_pages // gp),
        in_specs=[pl.BlockSpec((1, N_HEADS, Q_ROWS, HEAD_DIM), lambda b, g, pt: (b, 0, 0, 0)),
                  pl.BlockSpec((1, gp, S_ROWS, LANES), lambda b, g, pt: (b, g, 0, 0)),
                  pl.BlockSpec((1, 1, S_ROWS, LANES), lambda b, g, pt: (b, n_pages, 0, 0)),
                  pl.BlockSpec((1, N_HEADS, HEAD_DIM, LANES), lambda b, g, pt: (b, 0, 0, 0)),
                  pl.BlockSpec((1, N_HEADS, HEAD_DIM, LANES), lambda b, g, pt: (b, 0, 0, 0)),
                  pl.BlockSpec(memory_space=pl.ANY),
                  pl.BlockSpec(memory_space=pl.ANY)],
        out_specs=pl.BlockSpec((1, N_HEADS, S_ROWS, HEAD_DIM), lambda b, g, pt: (b, 0, 0, 0)),
        scratch_shapes=[pltpu.VMEM((2, gp, N_HEADS, HEAD_DIM, PAGE_SIZE), F32),
                        pltpu.VMEM((2, gp, N_HEADS, HEAD_DIM, PAGE_SIZE), F32),
                        pltpu.SemaphoreType.DMA((2, 2)),
                        pltpu.VMEM((N_HEADS, S_ROWS, 1), F32),
                        pltpu.VMEM((N_HEADS, S_ROWS, 1), F32),
                        pltpu.VMEM((N_HEADS, S_ROWS, HEAD_DIM), F32)])
    return pl.pallas_call(
        functools.partial(_dense_sample_kernel, gp=gp),
        grid_spec=grid_spec,
        out_shape=jax.ShapeDtypeStruct((nb, N_HEADS, S_ROWS, HEAD_DIM), F32),
        compiler_params=_cparams(2),
        name="dense_sample",
    )(page_table, qh, bias_t, bias_t, knew_t, vnew_t, ck_t, cv_t)


def _ffn_kernel(x_ref, at_ref, yb_ref, gm_ref, wga_ref, wgb_ref, wpa_ref, wpb_ref, wo_ref, gf_ref,
                wua_ref, wub_ref, fcw_ref, fcb_ref, wd_ref, gfin_ref, buf_ref,
                y_ref, st_ref, x1_ref, h2_ref, acc_ref, us_ref, carry_ref, *, tm, fc, tps, sample_nb):
    i = pl.program_id(0)
    c = pl.program_id(1)
    nc = pl.num_programs(1)

    @pl.when(c == 0)
    def _():
        x = x_ref[...]
        h = _rms(x, gm_ref[...]).astype(BF16)
        ga = jnp.dot(h, wga_ref[...], preferred_element_type=F32)
        gb = jnp.dot(h, wgb_ref[...], preferred_element_type=F32)
        ya = jnp.dot(at_ref[...], wpa_ref[...], preferred_element_type=F32)
        yb = jnp.dot(yb_ref[...], wpb_ref[...], preferred_element_type=F32)
        mix = _sigmoid(ga) * ya + _sigmoid(gb) * yb
        x1 = x + jnp.dot(mix.astype(BF16), wo_ref[...], preferred_element_type=F32)
        x1_ref[...] = x1
        h2_ref[...] = _rms(x1, gf_ref[...]).astype(BF16)
        acc_ref[...] = jnp.zeros(acc_ref.shape, F32)

    h2 = h2_ref[...]
    ua = jnp.dot(h2, wua_ref[...], preferred_element_type=F32)
    ub = jnp.dot(h2, wub_ref[...], preferred_element_type=F32)
    w = fcw_ref[...]
    if sample_nb:
        nb = sample_nb
        us = jnp.concatenate([buf_ref[...], ua], axis=0)
        uc = fcb_ref[...] + us[0:tm] * w[0:1]
        for j in range(1, FFN_CONV_W):
            uc = uc + us[j * nb:j * nb + tm] * w[j:j + 1]
        st_ref[...] = us[tm:tm + (FFN_CONV_W - 1) * nb]
    else:
        first = (i % tps) == 0
        us_ref[0:8, :] = jnp.where(first, jnp.zeros((8, fc), F32), carry_ref[c])
        us_ref[8:8 + tm, :] = ua
        uc = fcb_ref[...] + us_ref[6:6 + tm, :] * w[0:1]
        uc = uc + us_ref[7:7 + tm, :] * w[1:2]
        uc = uc + ua * w[2:3]
        tail = ua[tm - 8:tm, :]
        carry_ref[c] = tail
        st_ref[0] = tail
    act = (_gelu(uc) * ub).astype(BF16)
    acc_ref[...] += jnp.dot(act, wd_ref[...], preferred_element_type=F32)

    @pl.when(c == nc - 1)
    def _():
        y_ref[...] = _rms(x1_ref[...] + acc_ref[...], gfin_ref[...])


def _ffn_call(x2d, attn, yb, gm, wga, wgb, wpa, wpb, wo, gf, wup, fcw, fcb, wd, gfin, buf, *,
              tm, fc, seq, sample_nb):
    n, d = x2d.shape
    d_ff = wd.shape[0]
    nc = d_ff // fc
    tps = seq // tm if not sample_nb else 1
    tok = lambda w: pl.BlockSpec((tm, w), lambda i, c: (i, 0))
    const = lambda a: pl.BlockSpec(a.shape, lambda i, c: (0,) * a.ndim)
    if sample_nb:
        nst = (FFN_CONV_W - 1) * sample_nb
        buf_spec = pl.BlockSpec((nst, fc), lambda i, c: (0, c))
        st_spec = pl.BlockSpec((nst, fc), lambda i, c: (0, c))
        st_shape = jax.ShapeDtypeStruct((nst, d_ff), F32)
    else:
        buf_spec = pl.BlockSpec((8, LANES), lambda i, c: (0, 0))
        st_spec = pl.BlockSpec((1, 8, fc), lambda i, c: (i, 0, c))
        st_shape = jax.ShapeDtypeStruct((n // tm, 8, d_ff), F32)
    return pl.pallas_call(
        functools.partial(_ffn_kernel, tm=tm, fc=fc, tps=tps, sample_nb=sample_nb),
        grid=(n // tm, nc),
        in_specs=[tok(d), tok(ATTN_WIDTH), tok(LRU_WIDTH), const(gm), const(wga), const(wgb), const(wpa),
                  const(wpb), const(wo), const(gf),
                  pl.BlockSpec((d, fc), lambda i, c: (0, c)),
                  pl.BlockSpec((d, fc), lambda i, c: (0, nc + c)),
                  pl.BlockSpec((FFN_CONV_W, fc), lambda i, c: (0, c)),
                  pl.BlockSpec((1, fc), lambda i, c: (0, c)),
                  pl.BlockSpec((fc, d), lambda i, c: (c, 0)),
                  const(gfin), buf_spec],
        out_specs=(tok(d), st_spec),
        out_shape=(jax.ShapeDtypeStruct((n, d), F32), st_shape),
        scratch_shapes=[pltpu.VMEM((tm, d), F32), pltpu.VMEM((tm, d), BF16), pltpu.VMEM((tm, d), F32),
                        pltpu.VMEM((tm + 8, fc), F32), pltpu.VMEM((nc, 8, fc), F32)],
        compiler_params=_cparams(2),
        name="ffn_sample" if sample_nb else "ffn_prompt",
    )(x2d, attn, yb, gm, wga, wgb, wpa, wpb, wo, gf, wup, wup, fcw, fcb, wd, gfin, buf)


def _rope_tables(pos):
    half = HEAD_DIM // 2
    inv = jnp.power(ROPE_THETA, -jnp.arange(half, dtype=F32) / half)
    ang = pos.astype(F32)[:, None] * inv[None, :]
    cos, sin = jnp.cos(ang), jnp.sin(ang)
    z = jnp.zeros_like(sin)
    tile = lambda a, b: jnp.concatenate([a, b, a, b], axis=1)
    return tile(cos, cos), tile(-sin, z), tile(z, sin), cos.T, sin.T


def _block_diag(w):
    nblk, bw, _ = w.shape
    eye = jnp.eye(nblk, dtype=w.dtype)
    return (eye[:, None, :, None] * w[:, :, None, :]).reshape(nblk * bw, nblk * bw)


def kernel(x_prompt, x_sample, cache_k, cache_v, cache_kidx, page_table, state_lru_conv, state_lru_h,
           state_ffn_conv, norm_mix_g, w_in, lru_conv_w, lru_conv_b, lru_wa, lru_ba, lru_wx, lru_bx,
           lru_lambda, w_proj_a, w_proj_b, w_out, norm_ffn_g, w_up, ffn_conv_w, ffn_conv_b, w_down,
           norm_final_g):
    depth = w_in.shape[0]
    assert depth == 1, "single-layer step"
    nbp, seq, d = x_prompt.shape
    nbs, nts, _ = x_sample.shape
    n_pages = page_table.shape[1]
    past = n_pages * PAGE_SIZE
    d_ff = w_down.shape[1]
    a = ATTN_WIDTH
    row = lambda v: v.reshape(1, -1)

    win = w_in[0]
    o_ki = 4 * a
    o_wi = o_ki + IDX_DIM
    o_xl = o_wi + IDX_HEADS
    o_gl = o_xl + LRU_WIDTH
    o_ga = o_gl + LRU_WIDTH
    o_gb = o_ga + d
    wbig = win[:, :4 * a].astype(BF16)
    wbig_p = jnp.concatenate([wbig[:, :2 * a], wbig[:, 3 * a:]], axis=1)
    wsm = jnp.pad(win[:, o_ki:o_xl], ((0, 0), (0, LANES - IDX_DIM - IDX_HEADS))).astype(BF16)
    wkvt = win[:, a:3 * a].T.astype(BF16)
    wxl = win[:, o_xl:o_gl].astype(BF16)
    wgl = win[:, o_gl:o_ga].astype(BF16)
    wga = win[:, o_ga:o_gb].astype(BF16)
    wgb = win[:, o_gb:o_gb + d].astype(BF16)
    wa_bd = _block_diag(lru_wa[0]).astype(BF16)
    wx_bd = _block_diag(lru_wx[0]).astype(BF16)
    wpa = w_proj_a[0].astype(BF16)
    wpb = w_proj_b[0].astype(BF16)
    wo = w_out[0].astype(BF16)
    wup = w_up[0].astype(BF16)
    wd = w_down[0].astype(BF16)
    gm, gf, gfin = row(norm_mix_g[0]), row(norm_ffn_g[0]), row(norm_final_g)
    lru_args = (lru_conv_w[0], row(lru_conv_b[0]), wa_bd, row(lru_ba[0]), wx_bd, row(lru_bx[0]),
                row(lru_lambda[0]))
    ffn_w = (gm, wga, wgb, wpa, wpb, wo, gf, wup, ffn_conv_w[0], row(ffn_conv_b[0]), wd, gfin)

    xp2 = x_prompt.reshape(nbp * seq, d)
    q_p, qi_p, kb_p, ktf_p, vtf_p, vt_p, kiwi_p, ki2_p = _qkv_call(
        xp2, gm, wbig_p, wsm, wkvt, *_rope_tables(jnp.arange(seq, dtype=I32)), tm=512, seq=seq, prompt=True)
    wit_p = jnp.swapaxes(kiwi_p[:, IDX_DIM:IDX_DIM + IDX_HEADS].reshape(nbp, seq, IDX_HEADS), 1, 2)
    sh3 = lambda t, w: t.reshape(nbp, seq, w)
    attn_p = _attn_prompt_call(sh3(qi_p, a), wit_p, sh3(ki2_p, LANES), sh3(q_p, a), sh3(kb_p, a), vt_p, tq=256)
    yb_p, lconv_p, lh_p = _lru_prompt_call(x_prompt, gm, wxl, wgl, *lru_args, tc=512)
    y_p, fconv_p = _ffn_call(xp2, attn_p.reshape(nbp * seq, a), yb_p.reshape(nbp * seq, LRU_WIDTH), *ffn_w,
                             jnp.zeros((8, LANES), F32), tm=512, fc=1024, seq=seq, sample_nb=0)

    ns = nbs * nts
    xs2 = jnp.swapaxes(x_sample, 0, 1).reshape(ns, d)
    pos_s = jnp.repeat(past + jnp.arange(nts, dtype=I32), nbs)
    q_s, qi_s, kf_s, vf_s, kiwi_s = _qkv_call(xs2, gm, wbig, wsm, wkvt, *_rope_tables(pos_s),
                                              tm=ns, seq=nts, prompt=False)
    bm = lambda t, w: jnp.swapaxes(t.reshape(nts, nbs, w), 0, 1)
    k_s4 = bm(kf_s, a).reshape(nbs, nts, N_HEADS, HEAD_DIM)
    v_s4 = bm(vf_s, a).reshape(nbs, nts, N_HEADS, HEAD_DIM)
    ki_s = bm(kiwi_s[:, :IDX_DIM], IDX_DIM)
    wi_s = bm(kiwi_s[:, IDX_DIM:IDX_DIM + IDX_HEADS], IDX_HEADS)
    qi_hq = jnp.swapaxes(bm(qi_s, a).reshape(nbs, nts, IDX_HEADS, IDX_DIM), 1, 2).reshape(
        nbs, IDX_HEADS * nts, IDX_DIM)
    wi_hq = jnp.broadcast_to(jnp.swapaxes(wi_s, 1, 2).reshape(nbs, IDX_HEADS * nts, 1),
                             (nbs, IDX_HEADS * nts, LANES))
    kinew = jnp.pad(ki_s.astype(BF16), ((0, 0), (0, LANES - nts), (0, 0)))
    kidx_t = jnp.transpose(cache_kidx, (0, 1, 3, 2))
    ck_t = jnp.transpose(cache_k, (0, 1, 3, 4, 2))
    cv_t = jnp.transpose(cache_v, (0, 1, 3, 4, 2))
    bias_s = _select_call(page_table, qi_hq, wi_hq, kinew, kidx_t, nt=nts)
    bias_t = jnp.pad(jnp.swapaxes(bias_s, 1, 2), ((0, 0), (0, 0), (0, S_ROWS - nts), (0, 0)))
    qh = jnp.swapaxes(bm(q_s, a).reshape(nbs, nts, N_HEADS, HEAD_DIM), 1, 2)
    qh = jnp.pad(qh, ((0, 0), (0, 0), (0, Q_ROWS - nts), (0, 0))).astype(BF16)
    new_t = lambda t: jnp.pad(jnp.transpose(t, (0, 2, 3, 1)), ((0, 0), (0, 0), (0, 0), (0, LANES - nts)))
    o_s = _dense_sample_call(page_table, qh, bias_t, new_t(k_s4), new_t(v_s4), ck_t, cv_t, gp=16)
    attn_s = jnp.transpose(o_s[:, :, :nts], (2, 0, 1, 3)).reshape(ns, a).astype(BF16)
    tmaj = lambda s: jnp.swapaxes(s, 0, 1).reshape(-1, s.shape[-1])
    yb_s, lconv_s, lh_s = _lru_sample_call(xs2, gm, wxl, wgl, *lru_args, tmaj(state_lru_conv[0]),
                                           state_lru_h[0], nb=nbs, nt=nts)
    y_s, fconv_s = _ffn_call(xs2, attn_s, yb_s, *ffn_w, tmaj(state_ffn_conv[0]),
                             tm=ns, fc=1024, seq=nts, sample_nb=nbs)
    bmaj = lambda t, r: jnp.swapaxes(t.reshape(r, nbs, t.shape[-1]), 0, 1)

    return (y_p.reshape(nbp, seq, d),
            bmaj(y_s, nts),
            jnp.transpose(ktf_p.reshape(nbp, N_HEADS, HEAD_DIM, seq), (0, 3, 1, 2))[None],
            jnp.transpose(vtf_p.reshape(nbp, N_HEADS, HEAD_DIM, seq), (0, 3, 1, 2))[None],
            kiwi_p[:, :IDX_DIM].reshape(1, nbp, seq, IDX_DIM),
            lconv_p[None],
            lh_p.reshape(1, nbp, LRU_WIDTH),
            fconv_p.reshape(nbp, -1, 8, d_ff)[:, -1, 8 - (FFN_CONV_W - 1):][None],
            k_s4[None],
            v_s4[None],
            ki_s[None],
            bmaj(lconv_s, LRU_CONV_W - 1)[None],
            lh_s[None],
            bmaj(fconv_s, FFN_CONV_W - 1)[None])
```

```python
import functools

import numpy as np
import jax
import jax.numpy as jnp
from jax import lax
from jax.experimental import pallas as pl
from jax.experimental.pallas import tpu as pltpu

F32 = jnp.float32
BF16 = jnp.bfloat16
I32 = jnp.int32

N_HEADS = 8
HEAD_DIM = 64
ATTN_WIDTH = N_HEADS * HEAD_DIM
IDX_HEADS = 8
IDX_DIM = 64
TOPK = 256
LRU_WIDTH = 512
LRU_BLOCKS = 8
LRU_CONV_W = 4
LRU_C = 8.0
FFN_CONV_W = 3
ROPE_THETA = 10000.0
EPS = 1e-6
PAGE_SIZE = 128

LANES = 128
INT_MIN = -2 ** 31
NEG_BIG = -1e30
NEG_INF = float("-inf")
KEY_LOWEST = INT_MIN + 2 ** 23
VMEM_LIMIT = 56 * 1024 * 1024

NT_DIMS = (((1,), (1,)), ((), ()))


def _cparams(n_axes):
    return pltpu.CompilerParams(dimension_semantics=("arbitrary",) * n_axes,
                                vmem_limit_bytes=VMEM_LIMIT)


def _rms(x, g):
    r = lax.rsqrt(jnp.mean(x * x, axis=-1, keepdims=True) + EPS)
    return x * r * g


def _gelu(x):
    c = np.float32(np.sqrt(2.0 / np.pi))
    return x * (0.5 * (1.0 + jnp.tanh(c * (x + np.float32(0.044715) * (x * x * x)))))


def _sigmoid(x):
    return 1.0 / (1.0 + jnp.exp(-x))


def _softplus(z):
    return jnp.maximum(z, 0.0) + jnp.log(1.0 + jnp.exp(-jnp.abs(z)))


def _key_to_f32(k):
    return pltpu.bitcast(k ^ ((k >> 31) & jnp.int32(0x7FFFFFFF)), F32)


def _exact_threshold(count_ge, shape, two_bits=False):
    def enough(trial):
        return count_ge(_key_to_f32(trial)) >= float(TOPK)

    def bisect(i, k):
        trial = k + lax.shift_left(jnp.int32(1), jnp.int32(31) - i)
        return jnp.where(enough(trial), trial, k)

    def bisect2(i, k):
        d2 = lax.shift_left(jnp.int32(1), jnp.int32(31) - 2 * i)
        d1 = lax.shift_left(jnp.int32(1), jnp.int32(30) - 2 * i)
        e1, e2, e3 = enough(k + d1), enough(k + d2), enough(k + d2 + d1)
        return k + jnp.where(e2, jnp.where(e3, d2 + d1, d2), jnp.where(e1, d1, 0))

    k0 = jnp.full(shape, INT_MIN, I32)
    k = lax.fori_loop(0, 16, bisect2, k0) if two_bits else lax.fori_loop(0, 32, bisect, k0)
    k = jnp.maximum(k, jnp.int32(KEY_LOWEST))
    return _key_to_f32(k), _key_to_f32(k + 1)


def _qkv_kernel(x_ref, g_ref, wbig_ref, wsm_ref, wkvt_ref, cos_ref, sa_ref, sb_ref, cost_ref, sint_ref,
                *out_refs, prompt):
    if prompt:
        q_ref, qi_ref, kb_ref, ktf_ref, vtf_ref, vt_ref, kiwi_ref, ki2_ref = out_refs
    else:
        q_ref, qi_ref, kf_ref, vf_ref, kiwi_ref = out_refs
    h = _rms(x_ref[...], g_ref[...]).astype(BF16)
    y = jnp.dot(h, wbig_ref[...], preferred_element_type=F32)
    ys = jnp.dot(h, wsm_ref[...], preferred_element_type=F32)
    cos, sa, sb = cos_ref[...], sa_ref[...], sb_ref[...]
    o_qi = (2 if prompt else 3) * ATTN_WIDTH

    def rope(t):
        return t * cos + pltpu.roll(t, 96, 1) * sa + pltpu.roll(t, 32, 1) * sb

    for j in range(ATTN_WIDTH // LANES):
        sl = slice(LANES * j, LANES * (j + 1))
        qj = rope(y[:, LANES * j:LANES * (j + 1)]) * 0.125
        q_ref[:, sl] = qj.astype(q_ref.dtype)
        kj = rope(y[:, ATTN_WIDTH + LANES * j:ATTN_WIDTH + LANES * (j + 1)])
        if prompt:
            kb_ref[:, sl] = kj.astype(BF16)
        else:
            kf_ref[:, sl] = kj
        qij = rope(y[:, o_qi + LANES * j:o_qi + LANES * (j + 1)]) * 0.125
        qi_ref[:, sl] = qij.astype(BF16)
    ysr = rope(ys)
    lane = lax.broadcasted_iota(I32, ys.shape, 1)
    kiwi_ref[...] = jnp.where(lane < IDX_DIM, ysr, ys * np.float32(IDX_HEADS ** -0.5))
    if not prompt:
        vf_ref[...] = y[:, 2 * ATTN_WIDTH:3 * ATTN_WIDTH]
        return
    ki2_ref[...] = jnp.where(lane < IDX_DIM, ysr, pltpu.roll(ysr, IDX_DIM, 1)).astype(BF16)
    kvt = lax.dot_general(wkvt_ref[...], h, NT_DIMS, preferred_element_type=F32)
    cos_t, sin_t = cost_ref[...], sint_ref[...]
    hh = HEAD_DIM // 2
    for hd in range(N_HEADS):
        x1 = kvt[HEAD_DIM * hd:HEAD_DIM * hd + hh]
        x2 = kvt[HEAD_DIM * hd + hh:HEAD_DIM * (hd + 1)]
        ktf_ref[0, HEAD_DIM * hd:HEAD_DIM * hd + hh, :] = x1 * cos_t - x2 * sin_t
        ktf_ref[0, HEAD_DIM * hd + hh:HEAD_DIM * (hd + 1), :] = x2 * cos_t + x1 * sin_t
    v_t = kvt[ATTN_WIDTH:2 * ATTN_WIDTH]
    vtf_ref[0] = v_t
    vt_ref[0] = v_t.astype(BF16)


def _qkv_call(x2d, g, wbig, wsm, wkvt, cos, sa, sb, cos_t, sin_t, *, tm, seq, prompt):
    n, d = x2d.shape
    nt = n // tm
    tps = seq // tm if prompt else 1
    nb = n // seq if prompt else 1
    tok = lambda w: pl.BlockSpec((tm, w), lambda i: (i, 0))
    const = lambda a: pl.BlockSpec(a.shape, lambda i: (0,) * a.ndim)
    tab = pl.BlockSpec((tm, LANES), lambda i: (i % tps, 0))
    tab_t = pl.BlockSpec((HEAD_DIM // 2, tm), lambda i: (0, i % tps))
    if prompt:
        seq_t = pl.BlockSpec((1, ATTN_WIDTH, tm), lambda i: (i // tps, 0, i % tps))
        out_shape = (jax.ShapeDtypeStruct((n, ATTN_WIDTH), BF16),
                     jax.ShapeDtypeStruct((n, ATTN_WIDTH), BF16),
                     jax.ShapeDtypeStruct((n, ATTN_WIDTH), BF16),
                     jax.ShapeDtypeStruct((nb, ATTN_WIDTH, seq), F32),
                     jax.ShapeDtypeStruct((nb, ATTN_WIDTH, seq), F32),
                     jax.ShapeDtypeStruct((nb, ATTN_WIDTH, seq), BF16),
                     jax.ShapeDtypeStruct((n, LANES), F32),
                     jax.ShapeDtypeStruct((n, LANES), BF16))
        out_specs = (tok(ATTN_WIDTH), tok(ATTN_WIDTH), tok(ATTN_WIDTH), seq_t, seq_t, seq_t,
                     tok(LANES), tok(LANES))
    else:
        out_shape = (jax.ShapeDtypeStruct((n, ATTN_WIDTH), F32),
                     jax.ShapeDtypeStruct((n, ATTN_WIDTH), BF16),
                     jax.ShapeDtypeStruct((n, ATTN_WIDTH), F32),
                     jax.ShapeDtypeStruct((n, ATTN_WIDTH), F32),
                     jax.ShapeDtypeStruct((n, LANES), F32))
        out_specs = (tok(ATTN_WIDTH), tok(ATTN_WIDTH), tok(ATTN_WIDTH), tok(ATTN_WIDTH), tok(LANES))
    return pl.pallas_call(
        functools.partial(_qkv_kernel, prompt=prompt),
        grid=(nt,),
        in_specs=[tok(d), const(g), const(wbig), const(wsm), const(wkvt), tab, tab, tab, tab_t, tab_t],
        out_specs=out_specs, out_shape=out_shape,
        compiler_params=_cparams(1),
        name="qkv_prompt" if prompt else "qkv_sample",
    )(x2d, g, wbig, wsm, wkvt, cos, sa, sb, cos_t, sin_t)


def _lru_gates(xc, wa_ref, ba_ref, wx_ref, bx_ref, lam_ref):
    xcb = xc.astype(BF16)
    r = _sigmoid(jnp.dot(xcb, wa_ref[...], preferred_element_type=F32) + ba_ref[...])
    i = _sigmoid(jnp.dot(xcb, wx_ref[...], preferred_element_type=F32) + bx_ref[...])
    log_a = -LRU_C * r * _softplus(-lam_ref[...])
    a = jnp.exp(log_a)
    mult = jnp.sqrt(1.0 - jnp.exp(2.0 * log_a))
    return a, mult * (i * xc)


def _lru_prompt_kernel(x_ref, g_ref, wxl_ref, wgl_ref, cw_ref, cb_ref, wa_ref, ba_ref, wx_ref, bx_ref,
                       lam_ref, y_ref, conv_ref, hlast_ref, xs_ref, a_ref, b_ref, hs_ref, hc_ref, *, tc):
    t = pl.program_id(1)

    @pl.when(t == 0)
    def _():
        xs_ref[0:8, :] = jnp.zeros((8, LRU_WIDTH), F32)
        hc_ref[...] = jnp.zeros(hc_ref.shape, F32)

    h = _rms(x_ref[0], g_ref[...]).astype(BF16)
    xl = jnp.dot(h, wxl_ref[...], preferred_element_type=F32)
    gl = jnp.dot(h, wgl_ref[...], preferred_element_type=F32)
    xs_ref[8:8 + tc, :] = xl
    cw = cw_ref[...]
    xc = cb_ref[...] + xs_ref[5:5 + tc, :] * cw[0:1]
    xc = xc + xs_ref[6:6 + tc, :] * cw[1:2]
    xc = xc + xs_ref[7:7 + tc, :] * cw[2:3]
    xc = xc + xl * cw[3:4]
    tail = xl[tc - 8:tc, :]
    xs_ref[0:8, :] = tail
    conv_ref[0] = tail[8 - (LRU_CONV_W - 1):8, :]

    a, bt = _lru_gates(xc, wa_ref, ba_ref, wx_ref, bx_ref, lam_ref)
    a_ref[...] = a
    b_ref[...] = bt

    def step(i, hp):
        hn = a_ref[pl.ds(i, 1), :] * hp + b_ref[pl.ds(i, 1), :]
        hs_ref[pl.ds(i, 1), :] = hn
        return hn

    hl = lax.fori_loop(0, tc, step, hc_ref[...], unroll=8)
    hc_ref[...] = hl
    hlast_ref[0] = hl
    y_ref[0] = (hs_ref[...] * _gelu(gl)).astype(BF16)


def _lru_prompt_call(x3d, g, wxl, wgl, cw, cb, wa, ba, wx, bx, lam, *, tc):
    nb, seq, d = x3d.shape
    const = lambda a: pl.BlockSpec(a.shape, lambda b, t: (0,) * a.ndim)
    return pl.pallas_call(
        functools.partial(_lru_prompt_kernel, tc=tc),
        grid=(nb, seq // tc),
        in_specs=[pl.BlockSpec((1, tc, d), lambda b, t: (b, t, 0))] +
                 [const(a) for a in (g, wxl, wgl, cw, cb, wa, ba, wx, bx, lam)],
        out_specs=(pl.BlockSpec((1, tc, LRU_WIDTH), lambda b, t: (b, t, 0)),
                   pl.BlockSpec((1, LRU_CONV_W - 1, LRU_WIDTH), lambda b, t: (b, 0, 0)),
                   pl.BlockSpec((1, 1, LRU_WIDTH), lambda b, t: (b, 0, 0))),
        out_shape=(jax.ShapeDtypeStruct((nb, seq, LRU_WIDTH), BF16),
                   jax.ShapeDtypeStruct((nb, LRU_CONV_W - 1, LRU_WIDTH), F32),
                   jax.ShapeDtypeStruct((nb, 1, LRU_WIDTH), F32)),
        scratch_shapes=[pltpu.VMEM((tc + 8, LRU_WIDTH), F32), pltpu.VMEM((tc, LRU_WIDTH), F32),
                        pltpu.VMEM((tc, LRU_WIDTH), F32), pltpu.VMEM((tc, LRU_WIDTH), F32),
                        pltpu.VMEM((1, LRU_WIDTH), F32)],
        compiler_params=_cparams(2),
        name="lru_prompt",
    )(x3d, g, wxl, wgl, cw, cb, wa, ba, wx, bx, lam)


def _lru_sample_kernel(x_ref, g_ref, wxl_ref, wgl_ref, cw_ref, cb_ref, wa_ref, ba_ref, wx_ref, bx_ref,
                       lam_ref, buf_ref, h0_ref, y_ref, conv_ref, hlast_ref, *, nb, nt):
    h = _rms(x_ref[...], g_ref[...]).astype(BF16)
    xl = jnp.dot(h, wxl_ref[...], preferred_element_type=F32)
    gl = jnp.dot(h, wgl_ref[...], preferred_element_type=F32)
    xx = jnp.concatenate([buf_ref[...], xl], axis=0)
    cw = cw_ref[...]
    n = nb * nt
    xc = cb_ref[...] + xx[0:n] * cw[0:1]
    for j in range(1, LRU_CONV_W):
        xc = xc + xx[j * nb:j * nb + n] * cw[j:j + 1]
    conv_ref[...] = xx[n:n + (LRU_CONV_W - 1) * nb]
    a, bt = _lru_gates(xc, wa_ref, ba_ref, wx_ref, bx_ref, lam_ref)
    hp = h0_ref[...]
    hs = []
    for t in range(nt):
        hp = a[t * nb:(t + 1) * nb] * hp + bt[t * nb:(t + 1) * nb]
        hs.append(hp)
    hlast_ref[...] = hp
    y_ref[...] = (jnp.concatenate(hs, axis=0) * _gelu(gl)).astype(BF16)


def _lru_sample_call(x2d, g, wxl, wgl, cw, cb, wa, ba, wx, bx, lam, buf, h0, *, nb, nt):
    n = nb * nt
    return pl.pallas_call(
        functools.partial(_lru_sample_kernel, nb=nb, nt=nt),
        out_shape=(jax.ShapeDtypeStruct((n, LRU_WIDTH), BF16),
                   jax.ShapeDtypeStruct(((LRU_CONV_W - 1) * nb, LRU_WIDTH), F32),
                   jax.ShapeDtypeStruct((nb, LRU_WIDTH), F32)),
        compiler_params=pltpu.CompilerParams(vmem_limit_bytes=VMEM_LIMIT),
        name="lru_sample",
    )(x2d, g, wxl, wgl, cw, cb, wa, ba, wx, bx, lam, buf, h0)


def _attn_prompt_kernel(qi_ref, wit_ref, ki2_ref, q_ref, k_ref, vt_ref, o_ref, sc_ref, bias_ref, *, tq, tk, seq):
    qb = pl.program_id(1)
    nk = qb + 1
    half = lax.broadcasted_iota(I32, (tq, LANES), 1) // HEAD_DIM
    kpos0 = lax.broadcasted_iota(I32, (tk, tq), 0)
    qpos = qb * tq + lax.broadcasted_iota(I32, (tk, tq), 1)

    def masked_pair(ref, h):
        pair = ref[0, :, LANES * (h // 2):LANES * (h // 2 + 1)]
        return jnp.where(half == (h % 2), pair, jnp.zeros_like(pair))

    qim = [masked_pair(qi_ref, h) for h in range(IDX_HEADS)]
    wit = wit_ref[0]

    def score_chunk(c, carry):
        off = pl.multiple_of(c * tk, tk)
        kc = ki2_ref[0, pl.ds(off, tk), :]
        acc = jnp.zeros((tk, tq), F32)
        for h in range(IDX_HEADS):
            s = lax.dot_general(kc, qim[h], NT_DIMS, preferred_element_type=F32)
            acc = acc + jnp.maximum(s, 0.0) * wit[h:h + 1, :]
        sc_ref[pl.ds(off, tk), :] = jnp.where(kpos0 + off <= qpos, acc, NEG_INF)
        return carry

    lax.fori_loop(0, nk, score_chunk, 0)

    def total(weight):
        def body(c, part):
            off = pl.multiple_of(c * tk, tk)
            w = weight(sc_ref[pl.ds(off, tk), :], off)
            return part + jnp.sum(w.reshape(tk // 8, 8, tq), axis=0)
        part = lax.fori_loop(0, nk, body, jnp.zeros((8, tq), F32))
        return jnp.sum(part, axis=0, keepdims=True)

    def count_ge(trial):
        return total(lambda sc, off: jnp.where(sc >= trial, 1.0, 0.0))

    t, t_next = _exact_threshold(count_ge, (1, tq))
    pos_bits = seq.bit_length()

    def tie_limit():
        need = float(TOPK) - count_ge(t_next)

        def ties_below(jt):
            def weight(sc, off):
                x = jnp.where(kpos0 + off < jt, sc, NEG_INF)
                return jnp.where(x >= t, 1.0, 0.0) - jnp.where(x >= t_next, 1.0, 0.0)
            return total(weight)

        def bis(i, j):
            jt = j + lax.shift_left(jnp.int32(1), jnp.int32(pos_bits - 1) - i)
            return jnp.where(ties_below(jt) <= need, jt, j)

        return lax.fori_loop(0, pos_bits, bis, jnp.zeros((1, tq), I32))

    jlim = lax.cond(jnp.max(count_ge(t)) > float(TOPK), tie_limit,
                    lambda: jnp.full((1, tq), 2 ** pos_bits - 1, I32))

    def write_bias(c, carry):
        off = pl.multiple_of(c * tk, tk)
        thr = jnp.where(kpos0 + off < jlim, t, t_next)
        bias_ref[pl.ds(off, tk), :] = jnp.where(sc_ref[pl.ds(off, tk), :] >= thr, 0.0, NEG_BIG)
        return carry

    lax.fori_loop(0, nk, write_bias, 0)

    qm = [masked_pair(q_ref, h) for h in range(N_HEADS)]

    def chunk(c, carry):
        off = pl.multiple_of(c * tk, tk)
        bias = bias_ref[pl.ds(off, tk), :]
        ss = []
        for h in range(N_HEADS):
            kc = k_ref[0, pl.ds(off, tk), LANES * (h // 2):LANES * (h // 2 + 1)]
            ss.append(lax.dot_general(kc, qm[h], NT_DIMS, preferred_element_type=F32))
        ps, stats = [], []
        for h in range(N_HEADS):
            m, l = carry[3 * h:3 * h + 2]
            s = ss[h] + bias
            m_new = jnp.maximum(m, jnp.max(s, axis=0, keepdims=True))
            p = jnp.exp(s - m_new)
            alpha = jnp.exp(m - m_new)
            stats.append((m_new, alpha * l + jnp.sum(p, axis=0, keepdims=True), alpha))
            ps.append(p.astype(BF16))
        new = []
        for h in range(N_HEADS):
            vt = vt_ref[0, HEAD_DIM * h:HEAD_DIM * (h + 1), pl.ds(off, tk)]
            m_new, l_new, alpha = stats[h]
            acc_new = alpha * carry[3 * h + 2] + jnp.dot(vt, ps[h], preferred_element_type=F32)
            new += [m_new, l_new, acc_new]
        return tuple(new)

    init = (jnp.full((1, tq), NEG_BIG, F32), jnp.zeros((1, tq), F32),
            jnp.zeros((HEAD_DIM, tq), F32)) * N_HEADS
    res = lax.fori_loop(0, nk, chunk, init)
    outs = [res[3 * h + 2] / res[3 * h + 1] for h in range(N_HEADS)]
    o_t = jnp.concatenate(outs, axis=0).astype(BF16)
    eye = (lax.broadcasted_iota(I32, (tq, tq), 0) == lax.broadcasted_iota(I32, (tq, tq), 1))
    eye = jnp.where(eye, 1.0, 0.0).astype(BF16)
    o_ref[0] = lax.dot_general(eye, o_t, NT_DIMS, preferred_element_type=F32).astype(BF16)


def _attn_prompt_call(qi, wit, ki2, q, k, vt, *, tq):
    nb, seq, _ = q.shape
    return pl.pallas_call(
        functools.partial(_attn_prompt_kernel, tq=tq, tk=tq, seq=seq),
        grid=(nb, seq // tq),
        in_specs=[pl.BlockSpec((1, tq, ATTN_WIDTH), lambda b, i: (b, i, 0)),
                  pl.BlockSpec((1, IDX_HEADS, tq), lambda b, i: (b, 0, i)),
                  pl.BlockSpec((1, seq, LANES), lambda b, i: (b, 0, 0)),
                  pl.BlockSpec((1, tq, ATTN_WIDTH), lambda b, i: (b, i, 0)),
                  pl.BlockSpec((1, seq, ATTN_WIDTH), lambda b, i: (b, 0, 0)),
                  pl.BlockSpec((1, ATTN_WIDTH, seq), lambda b, i: (b, 0, 0))],
        out_specs=pl.BlockSpec((1, tq, ATTN_WIDTH), lambda b, i: (b, i, 0)),
        out_shape=jax.ShapeDtypeStruct((nb, seq, ATTN_WIDTH), BF16),
        scratch_shapes=[pltpu.VMEM((seq, tq), F32), pltpu.VMEM((seq, tq), F32)],
        compiler_params=_cparams(2),
        name="attn_prompt",
    )(qi, wit, ki2, q, k, vt)


def _chunk_rows(n_pages):
    return -(-(n_pages + 1) // 8) * 8


def _select_kernel(pt_ref, qi_ref, wi_ref, kinew_ref, kidx_hbm, bias_ref, kbuf, sem, sc_ref, *,
                   n_pages, nt, group):
    b = pl.program_id(0)
    nb = pl.num_programs(0)
    slot = b % 2
    n_rows_sc = _chunk_rows(n_pages)

    def page_copy(bb, sl, p):
        return pltpu.make_async_copy(kidx_hbm.at[0, pt_ref[bb, p]], kbuf.at[sl, p], sem.at[sl])

    def fetch(bb, sl):
        def body(p, carry):
            page_copy(bb, sl, p).start()
            return carry
        lax.fori_loop(0, n_pages, body, 0, unroll=8)

    @pl.when(b == 0)
    def _():
        fetch(0, 0)

    @pl.when(b + 1 < nb)
    def _():
        fetch(b + 1, 1 - slot)

    pltpu.make_async_copy(kidx_hbm.at[0, pl.ds(0, n_pages)], kbuf.at[slot], sem.at[slot]).wait()

    qi = qi_ref[0]
    wi = wi_ref[0]
    n_rows = IDX_HEADS * nt

    def head_sum(s):
        e = s[0:8]
        for r in range(1, n_rows // 8):
            e = e + s[8 * r:8 * r + 8]
        return e[0:nt] + e[nt:2 * nt]

    wi_g = jnp.concatenate([wi] * group, axis=1)

    def score_group(gi, carry):
        row0 = gi * group
        kc = jnp.concatenate([kbuf[slot, row0 + j] for j in range(group)], axis=1).astype(BF16)
        s = jnp.dot(qi, kc, preferred_element_type=F32)
        sc = head_sum(jnp.maximum(s, 0.0) * wi_g)
        for q in range(nt):
            for j in range(group):
                sc_ref[q, pl.ds(row0 + j, 1), :] = sc[q:q + 1, LANES * j:LANES * (j + 1)]
        return carry

    lax.fori_loop(0, n_pages // group, score_group, 0)

    s_new = lax.dot_general(qi, kinew_ref[0], NT_DIMS, preferred_element_type=F32)
    sc_new = head_sum(jnp.maximum(s_new, 0.0) * wi)
    lane = lax.broadcasted_iota(I32, (nt, LANES), 1)
    qrow = lax.broadcasted_iota(I32, (nt, LANES), 0)
    sc_new = jnp.where(lane <= qrow, sc_new, NEG_INF)
    for q in range(nt):
        sc_ref[q, n_pages:n_pages + 1, :] = sc_new[q:q + 1, :]
        sc_ref[q, n_pages + 1:n_rows_sc, :] = jnp.full((n_rows_sc - n_pages - 1, LANES), NEG_INF, F32)

    def reduce3(x, op):
        return op(op(x, axis=1, keepdims=True), axis=2, keepdims=True)

    def count_ge(trial, limit=None):
        sc = sc_ref[...]
        if limit is not None:
            sc = jnp.where(pos < limit, sc, NEG_INF)
        return reduce3(jnp.where(sc >= trial, 1.0, 0.0), jnp.sum)

    scores = sc_ref[...]
    pos = (lax.broadcasted_iota(I32, scores.shape, 1) * LANES + lax.broadcasted_iota(I32, scores.shape, 2))
    t, t_next = _exact_threshold(count_ge, (nt, 1, 1), two_bits=True)

    pos_bits = (n_rows_sc * LANES).bit_length()

    def tie_limit():
        need = float(TOPK) - count_ge(t_next)

        def bis(i, j):
            jt = j + lax.shift_left(jnp.int32(1), jnp.int32(pos_bits - 1) - i)
            return jnp.where(count_ge(t, jt) - count_ge(t_next, jt) <= need, jt, j)

        return lax.fori_loop(0, pos_bits, bis, jnp.zeros((nt, 1, 1), I32))

    jlim = lax.cond(jnp.max(count_ge(t)) > float(TOPK), tie_limit,
                    lambda: jnp.full((nt, 1, 1), 2 ** pos_bits - 1, I32))
    thr = jnp.where(pos < jlim, t, t_next)
    bias_ref[0] = jnp.where(scores >= thr, 0.0, NEG_BIG)


def _select_call(page_table, qi_s, wi_s, kinew, kidx_t, *, nt):
    nb, n_pages = page_table.shape
    rows = _chunk_rows(n_pages)
    grid_spec = pltpu.PrefetchScalarGridSpec(
        num_scalar_prefetch=1,
        grid=(nb,),
        in_specs=[pl.BlockSpec((1, IDX_HEADS * nt, IDX_DIM), lambda b, pt: (b, 0, 0)),
                  pl.BlockSpec((1, IDX_HEADS * nt, LANES), lambda b, pt: (b, 0, 0)),
                  pl.BlockSpec((1, LANES, IDX_DIM), lambda b, pt: (b, 0, 0)),
                  pl.BlockSpec(memory_space=pl.ANY)],
        out_specs=pl.BlockSpec((1, nt, rows, LANES), lambda b, pt: (b, 0, 0, 0)),
        scratch_shapes=[pltpu.VMEM((2, n_pages, IDX_DIM, PAGE_SIZE), F32),
                        pltpu.SemaphoreType.DMA((2,)),
                        pltpu.VMEM((nt, rows, LANES), F32)])
    return pl.pallas_call(
        functools.partial(_select_kernel, n_pages=n_pages, nt=nt, group=16),
        grid_spec=grid_spec,
        out_shape=jax.ShapeDtypeStruct((nb, nt, rows, LANES), F32),
        compiler_params=_cparams(1),
        name="select_sample",
    )(page_table, qi_s, wi_s, kinew, kidx_t)


Q_ROWS = 16
S_ROWS = 8


def _dense_sample_kernel(pt_ref, q_ref, bias_ref, biasn_ref, knew_ref, vnew_ref, ck_hbm, cv_hbm, o_ref,
                         kb, vb, sem, m_ref, l_ref, acc_ref, *, gp):
    b = pl.program_id(0)
    g = pl.program_id(1)
    ng = pl.num_programs(1)
    n_steps = pl.num_programs(0) * ng
    step = b * ng + g
    slot = step % 2

    def page_copies(st, sl, j):
        page = pt_ref[lax.div(st, ng), lax.rem(st, ng) * gp + j]
        return (pltpu.make_async_copy(ck_hbm.at[0, page], kb.at[sl, j], sem.at[0, sl]),
                pltpu.make_async_copy(cv_hbm.at[0, page], vb.at[sl, j], sem.at[1, sl]))

    def fetch(st, sl):
        for j in range(gp):
            for cp in page_copies(st, sl, j):
                cp.start()

    @pl.when(step == 0)
    def _():
        fetch(0, 0)

    @pl.when(step + 1 < n_steps)
    def _():
        fetch(step + 1, 1 - slot)

    for j in range(gp):
        for cp in page_copies(step, slot, j):
            cp.wait()

    @pl.when(g == 0)
    def _():
        m_ref[...] = jnp.full(m_ref.shape, NEG_BIG, F32)
        l_ref[...] = jnp.zeros(l_ref.shape, F32)
        acc_ref[...] = jnp.zeros(acc_ref.shape, F32)

    def attend(k_of, v_of, bias):
        ss = [jnp.dot(q_ref[0, h], k_of(h), preferred_element_type=F32)[0:S_ROWS] for h in range(N_HEADS)]
        ps, alphas = [], []
        for h in range(N_HEADS):
            s = ss[h] + bias
            m_old = m_ref[h]
            m_new = jnp.maximum(m_old, jnp.max(s, axis=1, keepdims=True))
            p = jnp.exp(s - m_new)
            alpha = jnp.exp(m_old - m_new)
            l_ref[h] = alpha * l_ref[h] + jnp.sum(p, axis=1, keepdims=True)
            m_ref[h] = m_new
            ps.append(jnp.concatenate([p, jnp.zeros((Q_ROWS - S_ROWS, p.shape[1]), F32)], axis=0).astype(BF16))
            alphas.append(alpha)
        for h in range(N_HEADS):
            pv = lax.dot_general(ps[h], v_of(h), NT_DIMS, preferred_element_type=F32)[0:S_ROWS]
            acc_ref[h] = alphas[h] * acc_ref[h] + pv

    def cat(buf, h):
        return jnp.concatenate([buf[slot, j, h] for j in range(gp)], axis=1).astype(BF16)

    attend(lambda h: cat(kb, h), lambda h: cat(vb, h),
           jnp.concatenate([bias_ref[0, j] for j in range(gp)], axis=1))

    @pl.when(g == ng - 1)
    def _():
        attend(lambda h: knew_ref[0, h].astype(BF16), lambda h: vnew_ref[0, h].astype(BF16), biasn_ref[0, 0])
        for h in range(N_HEADS):
            o_ref[0, h] = acc_ref[h] / l_ref[h]


def _dense_sample_call(page_table, qh, bias_t, knew_t, vnew_t, ck_t, cv_t, *, gp):
    nb, n_pages = page_table.shape
    grid_spec = pltpu.PrefetchScalarGridSpec(
        num_scalar_prefetch=1,
        grid=(nb, n_pages // gp),
        in_specs=[pl.BlockSpec((1, N_HEADS, Q_ROWS, HEAD_DIM), lambda b, g, pt: (b, 0, 0, 0)),
                  pl.BlockSpec((1, gp, S_ROWS, LANES), lambda b, g, pt: (b, g, 0, 0)),
                  pl.BlockSpec((1, 1, S_ROWS, LANES), lambda b, g, pt: (b, n_pages, 0, 0)),
                  pl.BlockSpec((1, N_HEADS, HEAD_DIM, LANES), lambda b, g, pt: (b, 0, 0, 0)),
                  pl.BlockSpec((1, N_HEADS, HEAD_DIM, LANES), lambda b, g, pt: (b, 0, 0, 0)),
                  pl.BlockSpec(memory_space=pl.ANY),
                  pl.BlockSpec(memory_space=pl.ANY)],
        out_specs=pl.BlockSpec((1, N_HEADS, S_ROWS, HEAD_DIM), lambda b, g, pt: (b, 0, 0, 0)),
        scratch_shapes=[pltpu.VMEM((2, gp, N_HEADS, HEAD_DIM, PAGE_SIZE), F32),
                        pltpu.VMEM((2, gp, N_HEADS, HEAD_DIM, PAGE_SIZE), F32),
                        pltpu.SemaphoreType.DMA((2, 2)),
                        pltpu.VMEM((N_HEADS, S_ROWS, 1), F32),
                        pltpu.VMEM((N_HEADS, S_ROWS, 1), F32),
                        pltpu.VMEM((N_HEADS, S_ROWS, HEAD_DIM), F32)])
    return pl.pallas_call(
        functools.partial(_dense_sample_kernel, gp=gp),
        grid_spec=grid_spec,
        out_shape=jax.ShapeDtypeStruct((nb, N_HEADS, S_ROWS, HEAD_DIM), F32),
        compiler_params=_cparams(2),
        name="dense_sample",
    )(page_table, qh, bias_t, bias_t, knew_t, vnew_t, ck_t, cv_t)


def _ffn_kernel(x_ref, at_ref, yb_ref, gm_ref, wga_ref, wgb_ref, wpa_ref, wpb_ref, wo_ref, gf_ref,
                wua_ref, wub_ref, fcw_ref, fcb_ref, wd_ref, gfin_ref, buf_ref,
                y_ref, st_ref, x1_ref, h2_ref, acc_ref, us_ref, carry_ref, *, tm, fc, tps, sample_nb):
    i = pl.program_id(0)
    c = pl.program_id(1)
    nc = pl.num_programs(1)

    @pl.when(c == 0)
    def _():
        x = x_ref[...]
        h = _rms(x, gm_ref[...]).astype(BF16)
        ga = jnp.dot(h, wga_ref[...], preferred_element_type=F32)
        gb = jnp.dot(h, wgb_ref[...], preferred_element_type=F32)
        ya = jnp.dot(at_ref[...], wpa_ref[...], preferred_element_type=F32)
        yb = jnp.dot(yb_ref[...], wpb_ref[...], preferred_element_type=F32)
        mix = _sigmoid(ga) * ya + _sigmoid(gb) * yb
        x1 = x + jnp.dot(mix.astype(BF16), wo_ref[...], preferred_element_type=F32)
        x1_ref[...] = x1
        h2_ref[...] = _rms(x1, gf_ref[...]).astype(BF16)
        acc_ref[...] = jnp.zeros(acc_ref.shape, F32)

    h2 = h2_ref[...]
    ua = jnp.dot(h2, wua_ref[...], preferred_element_type=F32)
    ub = jnp.dot(h2, wub_ref[...], preferred_element_type=F32)
    w = fcw_ref[...]
    if sample_nb:
        nb = sample_nb
        us = jnp.concatenate([buf_ref[...], ua], axis=0)
        uc = fcb_ref[...] + us[0:tm] * w[0:1]
        for j in range(1, FFN_CONV_W):
            uc = uc + us[j * nb:j * nb + tm] * w[j:j + 1]
        st_ref[...] = us[tm:tm + (FFN_CONV_W - 1) * nb]
    else:
        first = (i % tps) == 0
        us_ref[0:8, :] = jnp.where(first, jnp.zeros((8, fc), F32), carry_ref[c])
        us_ref[8:8 + tm, :] = ua
        uc = fcb_ref[...] + us_ref[6:6 + tm, :] * w[0:1]
        uc = uc + us_ref[7:7 + tm, :] * w[1:2]
        uc = uc + ua * w[2:3]
        tail = ua[tm - 8:tm, :]
        carry_ref[c] = tail
        st_ref[0] = tail
    act = (_gelu(uc) * ub).astype(BF16)
    acc_ref[...] += jnp.dot(act, wd_ref[...], preferred_element_type=F32)

    @pl.when(c == nc - 1)
    def _():
        y_ref[...] = _rms(x1_ref[...] + acc_ref[...], gfin_ref[...])


def _ffn_call(x2d, attn, yb, gm, wga, wgb, wpa, wpb, wo, gf, wup, fcw, fcb, wd, gfin, buf, *,
              tm, fc, seq, sample_nb):
    n, d = x2d.shape
    d_ff = wd.shape[0]
    nc = d_ff // fc
    tps = seq // tm if not sample_nb else 1
    tok = lambda w: pl.BlockSpec((tm, w), lambda i, c: (i, 0))
    const = lambda a: pl.BlockSpec(a.shape, lambda i, c: (0,) * a.ndim)
    if sample_nb:
        nst = (FFN_CONV_W - 1) * sample_nb
        buf_spec = pl.BlockSpec((nst, fc), lambda i, c: (0, c))
        st_spec = pl.BlockSpec((nst, fc), lambda i, c: (0, c))
        st_shape = jax.ShapeDtypeStruct((nst, d_ff), F32)
    else:
        buf_spec = pl.BlockSpec((8, LANES), lambda i, c: (0, 0))
        st_spec = pl.BlockSpec((1, 8, fc), lambda i, c: (i, 0, c))
        st_shape = jax.ShapeDtypeStruct((n // tm, 8, d_ff), F32)
    return pl.pallas_call(
        functools.partial(_ffn_kernel, tm=tm, fc=fc, tps=tps, sample_nb=sample_nb),
        grid=(n // tm, nc),
        in_specs=[tok(d), tok(ATTN_WIDTH), tok(LRU_WIDTH), const(gm), const(wga), const(wgb), const(wpa),
                  const(wpb), const(wo), const(gf),
                  pl.BlockSpec((d, fc), lambda i, c: (0, c)),
                  pl.BlockSpec((d, fc), lambda i, c: (0, nc + c)),
                  pl.BlockSpec((FFN_CONV_W, fc), lambda i, c: (0, c)),
                  pl.BlockSpec((1, fc), lambda i, c: (0, c)),
                  pl.BlockSpec((fc, d), lambda i, c: (c, 0)),
                  const(gfin), buf_spec],
        out_specs=(tok(d), st_spec),
        out_shape=(jax.ShapeDtypeStruct((n, d), F32), st_shape),
        scratch_shapes=[pltpu.VMEM((tm, d), F32), pltpu.VMEM((tm, d), BF16), pltpu.VMEM((tm, d), F32),
                        pltpu.VMEM((tm + 8, fc), F32), pltpu.VMEM((nc, 8, fc), F32)],
        compiler_params=_cparams(2),
        name="ffn_sample" if sample_nb else "ffn_prompt",
    )(x2d, attn, yb, gm, wga, wgb, wpa, wpb, wo, gf, wup, wup, fcw, fcb, wd, gfin, buf)


def _rope_tables(pos):
    half = HEAD_DIM // 2
    inv = jnp.power(ROPE_THETA, -jnp.arange(half, dtype=F32) / half)
    ang = pos.astype(F32)[:, None] * inv[None, :]
    cos, sin = jnp.cos(ang), jnp.sin(ang)
    z = jnp.zeros_like(sin)
    tile = lambda a, b: jnp.concatenate([a, b, a, b], axis=1)
    return tile(cos, cos), tile(-sin, z), tile(z, sin), cos.T, sin.T


def _block_diag(w):
    nblk, bw, _ = w.shape
    eye = jnp.eye(nblk, dtype=w.dtype)
    return (eye[:, None, :, None] * w[:, :, None, :]).reshape(nblk * bw, nblk * bw)


def kernel(x_prompt, x_sample, cache_k, cache_v, cache_kidx, page_table, state_lru_conv, state_lru_h,
           state_ffn_conv, norm_mix_g, w_in, lru_conv_w, lru_conv_b, lru_wa, lru_ba, lru_wx, lru_bx,
           lru_lambda, w_proj_a, w_proj_b, w_out, norm_ffn_g, w_up, ffn_conv_w, ffn_conv_b, w_down,
           norm_final_g):
    depth = w_in.shape[0]
    assert depth == 1, "single-layer step"
    nbp, seq, d = x_prompt.shape
    nbs, nts, _ = x_sample.shape
    n_pages = page_table.shape[1]
    past = n_pages * PAGE_SIZE
    d_ff = w_down.shape[1]
    a = ATTN_WIDTH
    row = lambda v: v.reshape(1, -1)

    win = w_in[0]
    o_ki = 4 * a
    o_wi = o_ki + IDX_DIM
    o_xl = o_wi + IDX_HEADS
    o_gl = o_xl + LRU_WIDTH
    o_ga = o_gl + LRU_WIDTH
    o_gb = o_ga + d
    wbig = win[:, :4 * a].astype(BF16)
    wbig_p = jnp.concatenate([wbig[:, :2 * a], wbig[:, 3 * a:]], axis=1)
    wsm = jnp.pad(win[:, o_ki:o_xl], ((0, 0), (0, LANES - IDX_DIM - IDX_HEADS))).astype(BF16)
    wkvt = win[:, a:3 * a].T.astype(BF16)
    wxl = win[:, o_xl:o_gl].astype(BF16)
    wgl = win[:, o_gl:o_ga].astype(BF16)
    wga = win[:, o_ga:o_gb].astype(BF16)
    wgb = win[:, o_gb:o_gb + d].astype(BF16)
    wa_bd = _block_diag(lru_wa[0]).astype(BF16)
    wx_bd = _block_diag(lru_wx[0]).astype(BF16)
    wpa = w_proj_a[0].astype(BF16)
    wpb = w_proj_b[0].astype(BF16)
    wo = w_out[0].astype(BF16)
    wup = w_up[0].astype(BF16)
    wd = w_down[0].astype(BF16)
    gm, gf, gfin = row(norm_mix_g[0]), row(norm_ffn_g[0]), row(norm_final_g)
    lru_args = (lru_conv_w[0], row(lru_conv_b[0]), wa_bd, row(lru_ba[0]), wx_bd, row(lru_bx[0]),
                row(lru_lambda[0]))
    ffn_w = (gm, wga, wgb, wpa, wpb, wo, gf, wup, ffn_conv_w[0], row(ffn_conv_b[0]), wd, gfin)

    xp2 = x_prompt.reshape(nbp * seq, d)
    q_p, qi_p, kb_p, ktf_p, vtf_p, vt_p, kiwi_p, ki2_p = _qkv_call(
        xp2, gm, wbig_p, wsm, wkvt, *_rope_tables(jnp.arange(seq, dtype=I32)), tm=512, seq=seq, prompt=True)
    wit_p = jnp.swapaxes(kiwi_p[:, IDX_DIM:IDX_DIM + IDX_HEADS].reshape(nbp, seq, IDX_HEADS), 1, 2)
    sh3 = lambda t, w: t.reshape(nbp, seq, w)
    attn_p = _attn_prompt_call(sh3(qi_p, a), wit_p, sh3(ki2_p, LANES), sh3(q_p, a), sh3(kb_p, a), vt_p, tq=256)
    yb_p, lconv_p, lh_p = _lru_prompt_call(x_prompt, gm, wxl, wgl, *lru_args, tc=512)
    y_p, fconv_p = _ffn_call(xp2, attn_p.reshape(nbp * seq, a), yb_p.reshape(nbp * seq, LRU_WIDTH), *ffn_w,
                             jnp.zeros((8, LANES), F32), tm=512, fc=1024, seq=seq, sample_nb=0)

    ns = nbs * nts
    xs2 = jnp.swapaxes(x_sample, 0, 1).reshape(ns, d)
    pos_s = jnp.repeat(past + jnp.arange(nts, dtype=I32), nbs)
    q_s, qi_s, kf_s, vf_s, kiwi_s = _qkv_call(xs2, gm, wbig, wsm, wkvt, *_rope_tables(pos_s),
                                              tm=ns, seq=nts, prompt=False)
    bm = lambda t, w: jnp.swapaxes(t.reshape(nts, nbs, w), 0, 1)
    k_s4 = bm(kf_s, a).reshape(nbs, nts, N_HEADS, HEAD_DIM)
    v_s4 = bm(vf_s, a).reshape(nbs, nts, N_HEADS, HEAD_DIM)
    ki_s = bm(kiwi_s[:, :IDX_DIM], IDX_DIM)
    wi_s = bm(kiwi_s[:, IDX_DIM:IDX_DIM + IDX_HEADS], IDX_HEADS)
    qi_hq = jnp.swapaxes(bm(qi_s, a).reshape(nbs, nts, IDX_HEADS, IDX_DIM), 1, 2).reshape(
        nbs, IDX_HEADS * nts, IDX_DIM)
    wi_hq = jnp.broadcast_to(jnp.swapaxes(wi_s, 1, 2).reshape(nbs, IDX_HEADS * nts, 1),
                             (nbs, IDX_HEADS * nts, LANES))
    kinew = jnp.pad(ki_s.astype(BF16), ((0, 0), (0, LANES - nts), (0, 0)))
    kidx_t = jnp.transpose(cache_kidx, (0, 1, 3, 2))
    ck_t = jnp.transpose(cache_k, (0, 1, 3, 4, 2))
    cv_t = jnp.transpose(cache_v, (0, 1, 3, 4, 2))
    bias_s = _select_call(page_table, qi_hq, wi_hq, kinew, kidx_t, nt=nts)
    bias_t = jnp.pad(jnp.swapaxes(bias_s, 1, 2), ((0, 0), (0, 0), (0, S_ROWS - nts), (0, 0)))
    qh = jnp.swapaxes(bm(q_s, a).reshape(nbs, nts, N_HEADS, HEAD_DIM), 1, 2)
    qh = jnp.pad(qh, ((0, 0), (0, 0), (0, Q_ROWS - nts), (0, 0))).astype(BF16)
    new_t = lambda t: jnp.pad(jnp.transpose(t, (0, 2, 3, 1)), ((0, 0), (0, 0), (0, 0), (0, LANES - nts)))
    o_s = _dense_sample_call(page_table, qh, bias_t, new_t(k_s4), new_t(v_s4), ck_t, cv_t, gp=16)
    attn_s = jnp.transpose(o_s[:, :, :nts], (2, 0, 1, 3)).reshape(ns, a).astype(BF16)
    tmaj = lambda s: jnp.swapaxes(s, 0, 1).reshape(-1, s.shape[-1])
    yb_s, lconv_s, lh_s = _lru_sample_call(xs2, gm, wxl, wgl, *lru_args, tmaj(state_lru_conv[0]),
                                           state_lru_h[0], nb=nbs, nt=nts)
    y_s, fconv_s = _ffn_call(xs2, attn_s, yb_s, *ffn_w, tmaj(state_ffn_conv[0]),
                             tm=ns, fc=1024, seq=nts, sample_nb=nbs)
    bmaj = lambda t, r: jnp.swapaxes(t.reshape(r, nbs, t.shape[-1]), 0, 1)

    return (y_p.reshape(nbp, seq, d),
            bmaj(y_s, nts),
            jnp.transpose(ktf_p.reshape(nbp, N_HEADS, HEAD_DIM, seq), (0, 3, 1, 2))[None],
            jnp.transpose(vtf_p.reshape(nbp, N_HEADS, HEAD_DIM, seq), (0, 3, 1, 2))[None],
            kiwi_p[:, :IDX_DIM].reshape(1, nbp, seq, IDX_DIM),
            lconv_p[None],
            lh_p.reshape(1, nbp, LRU_WIDTH),
            fconv_p.reshape(nbp, -1, 8, d_ff)[:, -1, 8 - (FFN_CONV_W - 1):][None],
            k_s4[None],
            v_s4[None],
            ki_s[None],
            bmaj(lconv_s, LRU_CONV_W - 1)[None],
            lh_s[None],
            bmaj(fconv_s, FFN_CONV_W - 1)[None])
```

```python
import functools

import numpy as np
import jax
import jax.numpy as jnp
from jax import lax
from jax.experimental import pallas as pl
from jax.experimental.pallas import tpu as pltpu

F32 = jnp.float32
BF16 = jnp.bfloat16
I32 = jnp.int32

N_HEADS = 8
HEAD_DIM = 64
ATTN_WIDTH = N_HEADS * HEAD_DIM
IDX_HEADS = 8
IDX_DIM = 64
TOPK = 256
LRU_WIDTH = 512
LRU_BLOCKS = 8
LRU_CONV_W = 4
LRU_C = 8.0
FFN_CONV_W = 3
ROPE_THETA = 10000.0
EPS = 1e-6
PAGE_SIZE = 128

LANES = 128
INT_MIN = -2 ** 31
NEG_BIG = -1e30
NEG_INF = float("-inf")
KEY_LOWEST = INT_MIN + 2 ** 23
VMEM_LIMIT = 56 * 1024 * 1024

NT_DIMS = (((1,), (1,)), ((), ()))


def _cparams(n_axes):
    return pltpu.CompilerParams(dimension_semantics=("arbitrary",) * n_axes,
                                vmem_limit_bytes=VMEM_LIMIT)


def _rms(x, g):
    r = lax.rsqrt(jnp.mean(x * x, axis=-1, keepdims=True) + EPS)
    return x * r * g


def _gelu(x):
    c = np.float32(np.sqrt(2.0 / np.pi))
    return x * (0.5 * (1.0 + jnp.tanh(c * (x + np.float32(0.044715) * (x * x * x)))))


def _sigmoid(x):
    return 1.0 / (1.0 + jnp.exp(-x))


def _softplus(z):
    return jnp.maximum(z, 0.0) + jnp.log(1.0 + jnp.exp(-jnp.abs(z)))


def _key_to_f32(k):
    return pltpu.bitcast(k ^ ((k >> 31) & jnp.int32(0x7FFFFFFF)), F32)


def _exact_threshold(count_ge, shape, two_bits=False):
    def enough(trial):
        return count_ge(_key_to_f32(trial)) >= float(TOPK)

    def bisect(i, k):
        trial = k + lax.shift_left(jnp.int32(1), jnp.int32(31) - i)
        return jnp.where(enough(trial), trial, k)

    def bisect2(i, k):
        d2 = lax.shift_left(jnp.int32(1), jnp.int32(31) - 2 * i)
        d1 = lax.shift_left(jnp.int32(1), jnp.int32(30) - 2 * i)
        e1, e2, e3 = enough(k + d1), enough(k + d2), enough(k + d2 + d1)
        return k + jnp.where(e2, jnp.where(e3, d2 + d1, d2), jnp.where(e1, d1, 0))

    k0 = jnp.full(shape, INT_MIN, I32)
    k = lax.fori_loop(0, 16, bisect2, k0) if two_bits else lax.fori_loop(0, 32, bisect, k0)
    k = jnp.maximum(k, jnp.int32(KEY_LOWEST))
    return _key_to_f32(k), _key_to_f32(k + 1)


def _qkv_kernel(x_ref, g_ref, wbig_ref, wsm_ref, wkvt_ref, cos_ref, sa_ref, sb_ref, cost_ref, sint_ref,
                *out_refs, prompt):
    if prompt:
        q_ref, qi_ref, kb_ref, ktf_ref, vtf_ref, vt_ref, kiwi_ref, ki2_ref = out_refs
    else:
        q_ref, qi_ref, kf_ref, vf_ref, kiwi_ref = out_refs
    h = _rms(x_ref[...], g_ref[...]).astype(BF16)
    y = jnp.dot(h, wbig_ref[...], preferred_element_type=F32)
    ys = jnp.dot(h, wsm_ref[...], preferred_element_type=F32)
    cos, sa, sb = cos_ref[...], sa_ref[...], sb_ref[...]
    o_qi = (2 if prompt else 3) * ATTN_WIDTH

    def rope(t):
        return t * cos + pltpu.roll(t, 96, 1) * sa + pltpu.roll(t, 32, 1) * sb

    for j in range(ATTN_WIDTH // LANES):
        sl = slice(LANES * j, LANES * (j + 1))
        qj = rope(y[:, LANES * j:LANES * (j + 1)]) * 0.125
        q_ref[:, sl] = qj.astype(q_ref.dtype)
        kj = rope(y[:, ATTN_WIDTH + LANES * j:ATTN_WIDTH + LANES * (j + 1)])
        if prompt:
            kb_ref[:, sl] = kj.astype(BF16)
        else:
            kf_ref[:, sl] = kj
        qij = rope(y[:, o_qi + LANES * j:o_qi + LANES * (j + 1)]) * 0.125
        qi_ref[:, sl] = qij.astype(BF16)
    ysr = rope(ys)
    lane = lax.broadcasted_iota(I32, ys.shape, 1)
    kiwi_ref[...] = jnp.where(lane < IDX_DIM, ysr, ys * np.float32(IDX_HEADS ** -0.5))
    if not prompt:
        vf_ref[...] = y[:, 2 * ATTN_WIDTH:3 * ATTN_WIDTH]
        return
    ki2_ref[...] = jnp.where(lane < IDX_DIM, ysr, pltpu.roll(ysr, IDX_DIM, 1)).astype(BF16)
    kvt = lax.dot_general(wkvt_ref[...], h, NT_DIMS, preferred_element_type=F32)
    cos_t, sin_t = cost_ref[...], sint_ref[...]
    hh = HEAD_DIM // 2
    for hd in range(N_HEADS):
        x1 = kvt[HEAD_DIM * hd:HEAD_DIM * hd + hh]
        x2 = kvt[HEAD_DIM * hd + hh:HEAD_DIM * (hd + 1)]
        ktf_ref[0, HEAD_DIM * hd:HEAD_DIM * hd + hh, :] = x1 * cos_t - x2 * sin_t
        ktf_ref[0, HEAD_DIM * hd + hh:HEAD_DIM * (hd + 1), :] = x2 * cos_t + x1 * sin_t
    v_t = kvt[ATTN_WIDTH:2 * ATTN_WIDTH]
    vtf_ref[0] = v_t
    vt_ref[0] = v_t.astype(BF16)


def _qkv_call(x2d, g, wbig, wsm, wkvt, cos, sa, sb, cos_t, sin_t, *, tm, seq, prompt):
    n, d = x2d.shape
    nt = n // tm
    tps = seq // tm if prompt else 1
    nb = n // seq if prompt else 1
    tok = lambda w: pl.BlockSpec((tm, w), lambda i: (i, 0))
    const = lambda a: pl.BlockSpec(a.shape, lambda i: (0,) * a.ndim)
    tab = pl.BlockSpec((tm, LANES), lambda i: (i % tps, 0))
    tab_t = pl.BlockSpec((HEAD_DIM // 2, tm), lambda i: (0, i % tps))
    if prompt:
        seq_t = pl.BlockSpec((1, ATTN_WIDTH, tm), lambda i: (i // tps, 0, i % tps))
        out_shape = (jax.ShapeDtypeStruct((n, ATTN_WIDTH), BF16),
                     jax.ShapeDtypeStruct((n, ATTN_WIDTH), BF16),
                     jax.ShapeDtypeStruct((n, ATTN_WIDTH), BF16),
                     jax.ShapeDtypeStruct((nb, ATTN_WIDTH, seq), F32),
                     jax.ShapeDtypeStruct((nb, ATTN_WIDTH, seq), F32),
                     jax.ShapeDtypeStruct((nb, ATTN_WIDTH, seq), BF16),
                     jax.ShapeDtypeStruct((n, LANES), F32),
                     jax.ShapeDtypeStruct((n, LANES), BF16))
        out_specs = (tok(ATTN_WIDTH), tok(ATTN_WIDTH), tok(ATTN_WIDTH), seq_t, seq_t, seq_t,
                     tok(LANES), tok(LANES))
    else:
        out_shape = (jax.ShapeDtypeStruct((n, ATTN_WIDTH), F32),
                     jax.ShapeDtypeStruct((n, ATTN_WIDTH), BF16),
                     jax.ShapeDtypeStruct((n, ATTN_WIDTH), F32),
                     jax.ShapeDtypeStruct((n, ATTN_WIDTH), F32),
                     jax.ShapeDtypeStruct((n, LANES), F32))
        out_specs = (tok(ATTN_WIDTH), tok(ATTN_WIDTH), tok(ATTN_WIDTH), tok(ATTN_WIDTH), tok(LANES))
    return pl.pallas_call(
        functools.partial(_qkv_kernel, prompt=prompt),
        grid=(nt,),
        in_specs=[tok(d), const(g), const(wbig), const(wsm), const(wkvt), tab, tab, tab, tab_t, tab_t],
        out_specs=out_specs, out_shape=out_shape,
        compiler_params=_cparams(1),
        name="qkv_prompt" if prompt else "qkv_sample",
    )(x2d, g, wbig, wsm, wkvt, cos, sa, sb, cos_t, sin_t)


def _lru_gates(xc, wa_ref, ba_ref, wx_ref, bx_ref, lam_ref):
    xcb = xc.astype(BF16)
    r = _sigmoid(jnp.dot(xcb, wa_ref[...], preferred_element_type=F32) + ba_ref[...])
    i = _sigmoid(jnp.dot(xcb, wx_ref[...], preferred_element_type=F32) + bx_ref[...])
    log_a = -LRU_C * r * _softplus(-lam_ref[...])
    a = jnp.exp(log_a)
    mult = jnp.sqrt(1.0 - jnp.exp(2.0 * log_a))
    return a, mult * (i * xc)


def _lru_prompt_kernel(x_ref, g_ref, wxl_ref, wgl_ref, cw_ref, cb_ref, wa_ref, ba_ref, wx_ref, bx_ref,
                       lam_ref, y_ref, conv_ref, hlast_ref, xs_ref, a_ref, b_ref, hs_ref, hc_ref, *, tc):
    t = pl.program_id(1)

    @pl.when(t == 0)
    def _():
        xs_ref[0:8, :] = jnp.zeros((8, LRU_WIDTH), F32)
        hc_ref[...] = jnp.zeros(hc_ref.shape, F32)

    h = _rms(x_ref[0], g_ref[...]).astype(BF16)
    xl = jnp.dot(h, wxl_ref[...], preferred_element_type=F32)
    gl = jnp.dot(h, wgl_ref[...], preferred_element_type=F32)
    xs_ref[8:8 + tc, :] = xl
    cw = cw_ref[...]
    xc = cb_ref[...] + xs_ref[5:5 + tc, :] * cw[0:1]
    xc = xc + xs_ref[6:6 + tc, :] * cw[1:2]
    xc = xc + xs_ref[7:7 + tc, :] * cw[2:3]
    xc = xc + xl * cw[3:4]
    tail = xl[tc - 8:tc, :]
    xs_ref[0:8, :] = tail
    conv_ref[0] = tail[8 - (LRU_CONV_W - 1):8, :]

    a, bt = _lru_gates(xc, wa_ref, ba_ref, wx_ref, bx_ref, lam_ref)
    a_ref[...] = a
    b_ref[...] = bt

    def step(i, hp):
        hn = a_ref[pl.ds(i, 1), :] * hp + b_ref[pl.ds(i, 1), :]
        hs_ref[pl.ds(i, 1), :] = hn
        return hn

    hl = lax.fori_loop(0, tc, step, hc_ref[...], unroll=8)
    hc_ref[...] = hl
    hlast_ref[0] = hl
    y_ref[0] = (hs_ref[...] * _gelu(gl)).astype(BF16)


def _lru_prompt_call(x3d, g, wxl, wgl, cw, cb, wa, ba, wx, bx, lam, *, tc):
    nb, seq, d = x3d.shape
    const = lambda a: pl.BlockSpec(a.shape, lambda b, t: (0,) * a.ndim)
    return pl.pallas_call(
        functools.partial(_lru_prompt_kernel, tc=tc),
        grid=(nb, seq // tc),
        in_specs=[pl.BlockSpec((1, tc, d), lambda b, t: (b, t, 0))] +
                 [const(a) for a in (g, wxl, wgl, cw, cb, wa, ba, wx, bx, lam)],
        out_specs=(pl.BlockSpec((1, tc, LRU_WIDTH), lambda b, t: (b, t, 0)),
                   pl.BlockSpec((1, LRU_CONV_W - 1, LRU_WIDTH), lambda b, t: (b, 0, 0)),
                   pl.BlockSpec((1, 1, LRU_WIDTH), lambda b, t: (b, 0, 0))),
        out_shape=(jax.ShapeDtypeStruct((nb, seq, LRU_WIDTH), BF16),
                   jax.ShapeDtypeStruct((nb, LRU_CONV_W - 1, LRU_WIDTH), F32),
                   jax.ShapeDtypeStruct((nb, 1, LRU_WIDTH), F32)),
        scratch_shapes=[pltpu.VMEM((tc + 8, LRU_WIDTH), F32), pltpu.VMEM((tc, LRU_WIDTH), F32),
                        pltpu.VMEM((tc, LRU_WIDTH), F32), pltpu.VMEM((tc, LRU_WIDTH), F32),
                        pltpu.VMEM((1, LRU_WIDTH), F32)],
        compiler_params=_cparams(2),
        name="lru_prompt",
    )(x3d, g, wxl, wgl, cw, cb, wa, ba, wx, bx, lam)


def _lru_sample_kernel(x_ref, g_ref, wxl_ref, wgl_ref, cw_ref, cb_ref, wa_ref, ba_ref, wx_ref, bx_ref,
                       lam_ref, buf_ref, h0_ref, y_ref, conv_ref, hlast_ref, *, nb, nt):
    h = _rms(x_ref[...], g_ref[...]).astype(BF16)
    xl = jnp.dot(h, wxl_ref[...], preferred_element_type=F32)
    gl = jnp.dot(h, wgl_ref[...], preferred_element_type=F32)
    xx = jnp.concatenate([buf_ref[...], xl], axis=0)
    cw = cw_ref[...]
    n = nb * nt
    xc = cb_ref[...] + xx[0:n] * cw[0:1]
    for j in range(1, LRU_CONV_W):
        xc = xc + xx[j * nb:j * nb + n] * cw[j:j + 1]
    conv_ref[...] = xx[n:n + (LRU_CONV_W - 1) * nb]
    a, bt = _lru_gates(xc, wa_ref, ba_ref, wx_ref, bx_ref, lam_ref)
    hp = h0_ref[...]
    hs = []
    for t in range(nt):
        hp = a[t * nb:(t + 1) * nb] * hp + bt[t * nb:(t + 1) * nb]
        hs.append(hp)
    hlast_ref[...] = hp
    y_ref[...] = (jnp.concatenate(hs, axis=0) * _gelu(gl)).astype(BF16)


def _lru_sample_call(x2d, g, wxl, wgl, cw, cb, wa, ba, wx, bx, lam, buf, h0, *, nb, nt):
    n = nb * nt
    return pl.pallas_call(
        functools.partial(_lru_sample_kernel, nb=nb, nt=nt),
        out_shape=(jax.ShapeDtypeStruct((n, LRU_WIDTH), BF16),
                   jax.ShapeDtypeStruct(((LRU_CONV_W - 1) * nb, LRU_WIDTH), F32),
                   jax.ShapeDtypeStruct((nb, LRU_WIDTH), F32)),
        compiler_params=pltpu.CompilerParams(vmem_limit_bytes=VMEM_LIMIT),
        name="lru_sample",
    )(x2d, g, wxl, wgl, cw, cb, wa, ba, wx, bx, lam, buf, h0)


def _attn_prompt_kernel(qi_ref, wit_ref, ki2_ref, q_ref, k_ref, vt_ref, o_ref, sc_ref, bias_ref, *, tq, tk, seq):
    qb = pl.program_id(1)
    nk = qb + 1
    half = lax.broadcasted_iota(I32, (tq, LANES), 1) // HEAD_DIM
    kpos0 = lax.broadcasted_iota(I32, (tk, tq), 0)
    qpos = qb * tq + lax.broadcasted_iota(I32, (tk, tq), 1)

    def masked_pair(ref, h):
        pair = ref[0, :, LANES * (h // 2):LANES * (h // 2 + 1)]
        return jnp.where(half == (h % 2), pair, jnp.zeros_like(pair))

    qim = [masked_pair(qi_ref, h) for h in range(IDX_HEADS)]
    wit = wit_ref[0]

    def score_chunk(c, carry):
        off = pl.multiple_of(c * tk, tk)
        kc = ki2_ref[0, pl.ds(off, tk), :]
        acc = jnp.zeros((tk, tq), F32)
        for h in range(IDX_HEADS):
            s = lax.dot_general(kc, qim[h], NT_DIMS, preferred_element_type=F32)
            acc = acc + jnp.maximum(s, 0.0) * wit[h:h + 1, :]
        sc_ref[pl.ds(off, tk), :] = jnp.where(kpos0 + off <= qpos, acc, NEG_INF)
        return carry

    lax.fori_loop(0, nk, score_chunk, 0)

    def total(weight):
        def body(c, part):
            off = pl.multiple_of(c * tk, tk)
            w = weight(sc_ref[pl.ds(off, tk), :], off)
            return part + jnp.sum(w.reshape(tk // 8, 8, tq), axis=0)
        part = lax.fori_loop(0, nk, body, jnp.zeros((8, tq), F32))
        return jnp.sum(part, axis=0, keepdims=True)

    def count_ge(trial):
        return total(lambda sc, off: jnp.where(sc >= trial, 1.0, 0.0))

    t, t_next = _exact_threshold(count_ge, (1, tq))
    pos_bits = seq.bit_length()

    def tie_limit():
        need = float(TOPK) - count_ge(t_next)

        def ties_below(jt):
            def weight(sc, off):
                x = jnp.where(kpos0 + off < jt, sc, NEG_INF)
                return jnp.where(x >= t, 1.0, 0.0) - jnp.where(x >= t_next, 1.0, 0.0)
            return total(weight)

        def bis(i, j):
            jt = j + lax.shift_left(jnp.int32(1), jnp.int32(pos_bits - 1) - i)
            return jnp.where(ties_below(jt) <= need, jt, j)

        return lax.fori_loop(0, pos_bits, bis, jnp.zeros((1, tq), I32))

    jlim = lax.cond(jnp.max(count_ge(t)) > float(TOPK), tie_limit,
                    lambda: jnp.full((1, tq), 2 ** pos_bits - 1, I32))

    def write_bias(c, carry):
        off = pl.multiple_of(c * tk, tk)
        thr = jnp.where(kpos0 + off < jlim, t, t_next)
        bias_ref[pl.ds(off, tk), :] = jnp.where(sc_ref[pl.ds(off, tk), :] >= thr, 0.0, NEG_BIG)
        return carry

    lax.fori_loop(0, nk, write_bias, 0)

    qm = [masked_pair(q_ref, h) for h in range(N_HEADS)]

    def chunk(c, carry):
        off = pl.multiple_of(c * tk, tk)
        bias = bias_ref[pl.ds(off, tk), :]
        ss = []
        for h in range(N_HEADS):
            kc = k_ref[0, pl.ds(off, tk), LANES * (h // 2):LANES * (h // 2 + 1)]
            ss.append(lax.dot_general(kc, qm[h], NT_DIMS, preferred_element_type=F32))
        ps, stats = [], []
        for h in range(N_HEADS):
            m, l = carry[3 * h:3 * h + 2]
            s = ss[h] + bias
            m_new = jnp.maximum(m, jnp.max(s, axis=0, keepdims=True))
            p = jnp.exp(s - m_new)
            alpha = jnp.exp(m - m_new)
            stats.append((m_new, alpha * l + jnp.sum(p, axis=0, keepdims=True), alpha))
            ps.append(p.astype(BF16))
        new = []
        for h in range(N_HEADS):
            vt = vt_ref[0, HEAD_DIM * h:HEAD_DIM * (h + 1), pl.ds(off, tk)]
            m_new, l_new, alpha = stats[h]
            acc_new = alpha * carry[3 * h + 2] + jnp.dot(vt, ps[h], preferred_element_type=F32)
            new += [m_new, l_new, acc_new]
        return tuple(new)

    init = (jnp.full((1, tq), NEG_BIG, F32), jnp.zeros((1, tq), F32),
            jnp.zeros((HEAD_DIM, tq), F32)) * N_HEADS
    res = lax.fori_loop(0, nk, chunk, init)
    outs = [res[3 * h + 2] / res[3 * h + 1] for h in range(N_HEADS)]
    o_t = jnp.concatenate(outs, axis=0).astype(BF16)
    eye = (lax.broadcasted_iota(I32, (tq, tq), 0) == lax.broadcasted_iota(I32, (tq, tq), 1))
    eye = jnp.where(eye, 1.0, 0.0).astype(BF16)
    o_ref[0] = lax.dot_general(eye, o_t, NT_DIMS, preferred_element_type=F32).astype(BF16)


def _attn_prompt_call(qi, wit, ki2, q, k, vt, *, tq):
    nb, seq, _ = q.shape
    return pl.pallas_call(
        functools.partial(_attn_prompt_kernel, tq=tq, tk=tq, seq=seq),
        grid=(nb, seq // tq),
        in_specs=[pl.BlockSpec((1, tq, ATTN_WIDTH), lambda b, i: (b, i, 0)),
                  pl.BlockSpec((1, IDX_HEADS, tq), lambda b, i: (b, 0, i)),
                  pl.BlockSpec((1, seq, LANES), lambda b, i: (b, 0, 0)),
                  pl.BlockSpec((1, tq, ATTN_WIDTH), lambda b, i: (b, i, 0)),
                  pl.BlockSpec((1, seq, ATTN_WIDTH), lambda b, i: (b, 0, 0)),
                  pl.BlockSpec((1, ATTN_WIDTH, seq), lambda b, i: (b, 0, 0))],
        out_specs=pl.BlockSpec((1, tq, ATTN_WIDTH), lambda b, i: (b, i, 0)),
        out_shape=jax.ShapeDtypeStruct((nb, seq, ATTN_WIDTH), BF16),
        scratch_shapes=[pltpu.VMEM((seq, tq), F32), pltpu.VMEM((seq, tq), F32)],
        compiler_params=_cparams(2),
        name="attn_prompt",
    )(qi, wit, ki2, q, k, vt)


def _chunk_rows(n_pages):
    return -(-(n_pages + 1) // 8) * 8


def _select_kernel(pt_ref, qi_ref, wi_ref, kinew_ref, kidx_hbm, bias_ref, kbuf, sem, sc_ref, *,
                   n_pages, nt, group):
    b = pl.program_id(0)
    nb = pl.num_programs(0)
    slot = b % 2
    n_rows_sc = _chunk_rows(n_pages)

    def page_copy(bb, sl, p):
        return pltpu.make_async_copy(kidx_hbm.at[0, pt_ref[bb, p]], kbuf.at[sl, p], sem.at[sl])

    def fetch(bb, sl):
        def body(p, carry):
            page_copy(bb, sl, p).start()
            return carry
        lax.fori_loop(0, n_pages, body, 0, unroll=8)

    @pl.when(b == 0)
    def _():
        fetch(0, 0)

    @pl.when(b + 1 < nb)
    def _():
        fetch(b + 1, 1 - slot)

    pltpu.make_async_copy(kidx_hbm.at[0, pl.ds(0, n_pages)], kbuf.at[slot], sem.at[slot]).wait()

    qi = qi_ref[0]
    wi = wi_ref[0]
    n_rows = IDX_HEADS * nt

    def head_sum(s):
        e = s[0:8]
        for r in range(1, n_rows // 8):
            e = e + s[8 * r:8 * r + 8]
        return e[0:nt] + e[nt:2 * nt]

    wi_g = jnp.concatenate([wi] * group, axis=1)

    def score_group(gi, carry):
        row0 = gi * group
        kc = jnp.concatenate([kbuf[slot, row0 + j] for j in range(group)], axis=1).astype(BF16)
        s = jnp.dot(qi, kc, preferred_element_type=F32)
        sc = head_sum(jnp.maximum(s, 0.0) * wi_g)
        for q in range(nt):
            for j in range(group):
                sc_ref[q, pl.ds(row0 + j, 1), :] = sc[q:q + 1, LANES * j:LANES * (j + 1)]
        return carry

    lax.fori_loop(0, n_pages // group, score_group, 0)

    s_new = lax.dot_general(qi, kinew_ref[0], NT_DIMS, preferred_element_type=F32)
    sc_new = head_sum(jnp.maximum(s_new, 0.0) * wi)
    lane = lax.broadcasted_iota(I32, (nt, LANES), 1)
    qrow = lax.broadcasted_iota(I32, (nt, LANES), 0)
    sc_new = jnp.where(lane <= qrow, sc_new, NEG_INF)
    for q in range(nt):
        sc_ref[q, n_pages:n_pages + 1, :] = sc_new[q:q + 1, :]
        sc_ref[q, n_pages + 1:n_rows_sc, :] = jnp.full((n_rows_sc - n_pages - 1, LANES), NEG_INF, F32)

    def reduce3(x, op):
        return op(op(x, axis=1, keepdims=True), axis=2, keepdims=True)

    def count_ge(trial, limit=None):
        sc = sc_ref[...]
        if limit is not None:
            sc = jnp.where(pos < limit, sc, NEG_INF)
        return reduce3(jnp.where(sc >= trial, 1.0, 0.0), jnp.sum)

    scores = sc_ref[...]
    pos = (lax.broadcasted_iota(I32, scores.shape, 1) * LANES + lax.broadcasted_iota(I32, scores.shape, 2))
    t, t_next = _exact_threshold(count_ge, (nt, 1, 1), two_bits=True)

    pos_bits = (n_rows_sc * LANES).bit_length()

    def tie_limit():
        need = float(TOPK) - count_ge(t_next)

        def bis(i, j):
            jt = j + lax.shift_left(jnp.int32(1), jnp.int32(pos_bits - 1) - i)
            return jnp.where(count_ge(t, jt) - count_ge(t_next, jt) <= need, jt, j)

        return lax.fori_loop(0, pos_bits, bis, jnp.zeros((nt, 1, 1), I32))

    jlim = lax.cond(jnp.max(count_ge(t)) > float(TOPK), tie_limit,
                    lambda: jnp.full((nt, 1, 1), 2 ** pos_bits - 1, I32))
    thr = jnp.where(pos < jlim, t, t_next)
    bias_ref[0] = jnp.where(scores >= thr, 0.0, NEG_BIG)


def _select_call(page_table, qi_s, wi_s, kinew, kidx_t, *, nt):
    nb, n_pages = page_table.shape
    rows = _chunk_rows(n_pages)
    grid_spec = pltpu.PrefetchScalarGridSpec(
        num_scalar_prefetch=1,
        grid=(nb,),
        in_specs=[pl.BlockSpec((1, IDX_HEADS * nt, IDX_DIM), lambda b, pt: (b, 0, 0)),
                  pl.BlockSpec((1, IDX_HEADS * nt, LANES), lambda b, pt: (b, 0, 0)),
                  pl.BlockSpec((1, LANES, IDX_DIM), lambda b, pt: (b, 0, 0)),
                  pl.BlockSpec(memory_space=pl.ANY)],
        out_specs=pl.BlockSpec((1, nt, rows, LANES), lambda b, pt: (b, 0, 0, 0)),
        scratch_shapes=[pltpu.VMEM((2, n_pages, IDX_DIM, PAGE_SIZE), F32),
                        pltpu.SemaphoreType.DMA((2,)),
                        pltpu.VMEM((nt, rows, LANES), F32)])
    return pl.pallas_call(
        functools.partial(_select_kernel, n_pages=n_pages, nt=nt, group=16),
        grid_spec=grid_spec,
        out_shape=jax.ShapeDtypeStruct((nb, nt, rows, LANES), F32),
        compiler_params=_cparams(1),
        name="select_sample",
    )(page_table, qi_s, wi_s, kinew, kidx_t)


Q_ROWS = 16
S_ROWS = 8


def _dense_sample_kernel(pt_ref, q_ref, bias_ref, biasn_ref, knew_ref, vnew_ref, ck_hbm, cv_hbm, o_ref,
                         kb, vb, sem, m_ref, l_ref, acc_ref, *, gp):
    b = pl.program_id(0)
    g = pl.program_id(1)
    ng = pl.num_programs(1)
    n_steps = pl.num_programs(0) * ng
    step = b * ng + g
    slot = step % 2

    def page_copies(st, sl, j):
        page = pt_ref[lax.div(st, ng), lax.rem(st, ng) * gp + j]
        return (pltpu.make_async_copy(ck_hbm.at[0, page], kb.at[sl, j], sem.at[0, sl]),
                pltpu.make_async_copy(cv_hbm.at[0, page], vb.at[sl, j], sem.at[1, sl]))

    def fetch(st, sl):
        for j in range(gp):
            for cp in page_copies(st, sl, j):
                cp.start()

    @pl.when(step == 0)
    def _():
        fetch(0, 0)

    @pl.when(step + 1 < n_steps)
    def _():
        fetch(step + 1, 1 - slot)

    for j in range(gp):
        for cp in page_copies(step, slot, j):
            cp.wait()

    @pl.when(g == 0)
    def _():
        m_ref[...] = jnp.full(m_ref.shape, NEG_BIG, F32)
        l_ref[...] = jnp.zeros(l_ref.shape, F32)
        acc_ref[...] = jnp.zeros(acc_ref.shape, F32)

    def attend(k_of, v_of, bias):
        ss = [jnp.dot(q_ref[0, h], k_of(h), preferred_element_type=F32)[0:S_ROWS] for h in range(N_HEADS)]
        ps, alphas = [], []
        for h in range(N_HEADS):
            s = ss[h] + bias
            m_old = m_ref[h]
            m_new = jnp.maximum(m_old, jnp.max(s, axis=1, keepdims=True))
            p = jnp.exp(s - m_new)
            alpha = jnp.exp(m_old - m_new)
            l_ref[h] = alpha * l_ref[h] + jnp.sum(p, axis=1, keepdims=True)
            m_ref[h] = m_new
            ps.append(jnp.concatenate([p, jnp.zeros((Q_ROWS - S_ROWS, p.shape[1]), F32)], axis=0).astype(BF16))
            alphas.append(alpha)
        for h in range(N_HEADS):
            pv = lax.dot_general(ps[h], v_of(h), NT_DIMS, preferred_element_type=F32)[0:S_ROWS]
            acc_ref[h] = alphas[h] * acc_ref[h] + pv

    def cat(buf, h):
        return jnp.concatenate([buf[slot, j, h] for j in range(gp)], axis=1).astype(BF16)

    attend(lambda h: cat(kb, h), lambda h: cat(vb, h),
           jnp.concatenate([bias_ref[0, j] for j in range(gp)], axis=1))

    @pl.when(g == ng - 1)
    def _():
        attend(lambda h: knew_ref[0, h].astype(BF16), lambda h: vnew_ref[0, h].astype(BF16), biasn_ref[0, 0])
        for h in range(N_HEADS):
            o_ref[0, h] = acc_ref[h] / l_ref[h]


def _dense_sample_call(page_table, qh, bias_t, knew_t, vnew_t, ck_t, cv_t, *, gp):
    nb, n_pages = page_table.shape
    grid_spec = pltpu.PrefetchScalarGridSpec(
        num_scalar_prefetch=1,
        grid=(nb, n_pages // gp),
        in_specs=[pl.BlockSpec((1, N_HEADS, Q_ROWS, HEAD_DIM), lambda b, g, pt: (b, 0, 0, 0)),
                  pl.BlockSpec((1, gp, S_ROWS, LANES), lambda b, g, pt: (b, g, 0, 0)),
                  pl.BlockSpec((1, 1, S_ROWS, LANES), lambda b, g, pt: (b, n_pages, 0, 0)),
                  pl.BlockSpec((1, N_HEADS, HEAD_DIM, LANES), lambda b, g, pt: (b, 0, 0, 0)),
                  pl.BlockSpec((1, N_HEADS, HEAD_DIM, LANES), lambda b, g, pt: (b, 0, 0, 0)),
                  pl.BlockSpec(memory_space=pl.ANY),
                  pl.BlockSpec(memory_space=pl.ANY)],
        out_specs=pl.BlockSpec((1, N_HEADS, S_ROWS, HEAD_DIM), lambda b, g, pt: (b, 0, 0, 0)),
        scratch_shapes=[pltpu.VMEM((2, gp, N_HEADS, HEAD_DIM, PAGE_SIZE), F32),
                        pltpu.VMEM((2, gp, N_HEADS, HEAD_DIM, PAGE_SIZE), F32),
                        pltpu.SemaphoreType.DMA((2, 2)),
                        pltpu.VMEM((N_HEADS, S_ROWS, 1), F32),
                        pltpu.VMEM((N_HEADS, S_ROWS, 1), F32),
                        pltpu.VMEM((N_HEADS, S_ROWS, HEAD_DIM), F32)])
    return pl.pallas_call(
        functools.partial(_dense_sample_kernel, gp=gp),
        grid_spec=grid_spec,
        out_shape=jax.ShapeDtypeStruct((nb, N_HEADS, S_ROWS, HEAD_DIM), F32),
        compiler_params=_cparams(2),
        name="dense_sample",
    )(page_table, qh, bias_t, bias_t, knew_t, vnew_t, ck_t, cv_t)


def _ffn_kernel(x_ref, at_ref, yb_ref, gm_ref, wga_ref, wgb_ref, wpa_ref, wpb_ref, wo_ref, gf_ref,
                wua_ref, wub_ref, fcw_ref, fcb_ref, wd_ref, gfin_ref, buf_ref,
                y_ref, st_ref, x1_ref, h2_ref, acc_ref, us_ref, carry_ref, *, tm, fc, tps, sample_nb):
    i = pl.program_id(0)
    c = pl.program_id(1)
    nc = pl.num_programs(1)

    @pl.when(c == 0)
    def _():
        x = x_ref[...]
        h = _rms(x, gm_ref[...]).astype(BF16)
        ga = jnp.dot(h, wga_ref[...], preferred_element_type=F32)
        gb = jnp.dot(h, wgb_ref[...], preferred_element_type=F32)
        ya = jnp.dot(at_ref[...], wpa_ref[...], preferred_element_type=F32)
        yb = jnp.dot(yb_ref[...], wpb_ref[...], preferred_element_type=F32)
        mix = _sigmoid(ga) * ya + _sigmoid(gb) * yb
        x1 = x + jnp.dot(mix.astype(BF16), wo_ref[...], preferred_element_type=F32)
        x1_ref[...] = x1
        h2_ref[...] = _rms(x1, gf_ref[...]).astype(BF16)
        acc_ref[...] = jnp.zeros(acc_ref.shape, F32)

    h2 = h2_ref[...]
    ua = jnp.dot(h2, wua_ref[...], preferred_element_type=F32)
    ub = jnp.dot(h2, wub_ref[...], preferred_element_type=F32)
    w = fcw_ref[...]
    if sample_nb:
        nb = sample_nb
        us = jnp.concatenate([buf_ref[...], ua], axis=0)
        uc = fcb_ref[...] + us[0:tm] * w[0:1]
        for j in range(1, FFN_CONV_W):
            uc = uc + us[j * nb:j * nb + tm] * w[j:j + 1]
        st_ref[...] = us[tm:tm + (FFN_CONV_W - 1) * nb]
    else:
        first = (i % tps) == 0
        us_ref[0:8, :] = jnp.where(first, jnp.zeros((8, fc), F32), carry_ref[c])
        us_ref[8:8 + tm, :] = ua
        uc = fcb_ref[...] + us_ref[6:6 + tm, :] * w[0:1]
        uc = uc + us_ref[7:7 + tm, :] * w[1:2]
        uc = uc + ua * w[2:3]
        tail = ua[tm - 8:tm, :]
        carry_ref[c] = tail
        st_ref[0] = tail
    act = (_gelu(uc) * ub).astype(BF16)
    acc_ref[...] += jnp.dot(act, wd_ref[...], preferred_element_type=F32)

    @pl.when(c == nc - 1)
    def _():
        y_ref[...] = _rms(x1_ref[...] + acc_ref[...], gfin_ref[...])


def _ffn_call(x2d, attn, yb, gm, wga, wgb, wpa, wpb, wo, gf, wup, fcw, fcb, wd, gfin, buf, *,
              tm, fc, seq, sample_nb):
    n, d = x2d.shape
    d_ff = wd.shape[0]
    nc = d_ff // fc
    tps = seq // tm if not sample_nb else 1
    tok = lambda w: pl.BlockSpec((tm, w), lambda i, c: (i, 0))
    const = lambda a: pl.BlockSpec(a.shape, lambda i, c: (0,) * a.ndim, pipeline_mode=pl.Buffered(1))
    wmode = pl.Buffered(1) if nc == 1 else None
    if sample_nb:
        nst = (FFN_CONV_W - 1) * sample_nb
        buf_spec = pl.BlockSpec((nst, fc), lambda i, c: (0, c))
        st_spec = pl.BlockSpec((nst, fc), lambda i, c: (0, c))
        st_shape = jax.ShapeDtypeStruct((nst, d_ff), F32)
    else:
        buf_spec = pl.BlockSpec((8, LANES), lambda i, c: (0, 0))
        st_spec = pl.BlockSpec((1, 8, fc), lambda i, c: (i, 0, c))
        st_shape = jax.ShapeDtypeStruct((n // tm, 8, d_ff), F32)
    return pl.pallas_call(
        functools.partial(_ffn_kernel, tm=tm, fc=fc, tps=tps, sample_nb=sample_nb),
        grid=(n // tm, nc),
        in_specs=[tok(d), tok(ATTN_WIDTH), tok(LRU_WIDTH), const(gm), const(wga), const(wgb), const(wpa),
                  const(wpb), const(wo), const(gf),
                  pl.BlockSpec((d, fc), lambda i, c: (0, c), pipeline_mode=wmode),
                  pl.BlockSpec((d, fc), lambda i, c: (0, nc + c), pipeline_mode=wmode),
                  pl.BlockSpec((FFN_CONV_W, fc), lambda i, c: (0, c)),
                  pl.BlockSpec((1, fc), lambda i, c: (0, c)),
                  pl.BlockSpec((fc, d), lambda i, c: (c, 0), pipeline_mode=wmode),
                  const(gfin), buf_spec],
        out_specs=(tok(d), st_spec),
        out_shape=(jax.ShapeDtypeStruct((n, d), F32), st_shape),
        scratch_shapes=[pltpu.VMEM((tm, d), F32), pltpu.VMEM((tm, d), BF16), pltpu.VMEM((tm, d), F32),
                        pltpu.VMEM((tm + 8, fc), F32), pltpu.VMEM((nc, 8, fc), F32)],
        compiler_params=_cparams(2),
        name="ffn_sample" if sample_nb else "ffn_prompt",
    )(x2d, attn, yb, gm, wga, wgb, wpa, wpb, wo, gf, wup, wup, fcw, fcb, wd, gfin, buf)


def _rope_tables(pos):
    half = HEAD_DIM // 2
    inv = jnp.power(ROPE_THETA, -jnp.arange(half, dtype=F32) / half)
    ang = pos.astype(F32)[:, None] * inv[None, :]
    cos, sin = jnp.cos(ang), jnp.sin(ang)
    z = jnp.zeros_like(sin)
    tile = lambda a, b: jnp.concatenate([a, b, a, b], axis=1)
    return tile(cos, cos), tile(-sin, z), tile(z, sin), cos.T, sin.T


def _block_diag(w):
    nblk, bw, _ = w.shape
    eye = jnp.eye(nblk, dtype=w.dtype)
    return (eye[:, None, :, None] * w[:, :, None, :]).reshape(nblk * bw, nblk * bw)


def kernel(x_prompt, x_sample, cache_k, cache_v, cache_kidx, page_table, state_lru_conv, state_lru_h,
           state_ffn_conv, norm_mix_g, w_in, lru_conv_w, lru_conv_b, lru_wa, lru_ba, lru_wx, lru_bx,
           lru_lambda, w_proj_a, w_proj_b, w_out, norm_ffn_g, w_up, ffn_conv_w, ffn_conv_b, w_down,
           norm_final_g):
    depth = w_in.shape[0]
    assert depth == 1, "single-layer step"
    nbp, seq, d = x_prompt.shape
    nbs, nts, _ = x_sample.shape
    n_pages = page_table.shape[1]
    past = n_pages * PAGE_SIZE
    d_ff = w_down.shape[1]
    a = ATTN_WIDTH
    row = lambda v: v.reshape(1, -1)

    win = w_in[0]
    o_ki = 4 * a
    o_wi = o_ki + IDX_DIM
    o_xl = o_wi + IDX_HEADS
    o_gl = o_xl + LRU_WIDTH
    o_ga = o_gl + LRU_WIDTH
    o_gb = o_ga + d
    wbig = win[:, :4 * a].astype(BF16)
    wbig_p = jnp.concatenate([wbig[:, :2 * a], wbig[:, 3 * a:]], axis=1)
    wsm = jnp.pad(win[:, o_ki:o_xl], ((0, 0), (0, LANES - IDX_DIM - IDX_HEADS))).astype(BF16)
    wkvt = win[:, a:3 * a].T.astype(BF16)
    wxl = win[:, o_xl:o_gl].astype(BF16)
    wgl = win[:, o_gl:o_ga].astype(BF16)
    wga = win[:, o_ga:o_gb].astype(BF16)
    wgb = win[:, o_gb:o_gb + d].astype(BF16)
    wa_bd = _block_diag(lru_wa[0]).astype(BF16)
    wx_bd = _block_diag(lru_wx[0]).astype(BF16)
    wpa = w_proj_a[0].astype(BF16)
    wpb = w_proj_b[0].astype(BF16)
    wo = w_out[0].astype(BF16)
    wup = w_up[0].astype(BF16)
    wd = w_down[0].astype(BF16)
    gm, gf, gfin = row(norm_mix_g[0]), row(norm_ffn_g[0]), row(norm_final_g)
    lru_args = (lru_conv_w[0], row(lru_conv_b[0]), wa_bd, row(lru_ba[0]), wx_bd, row(lru_bx[0]),
                row(lru_lambda[0]))
    ffn_w = (gm, wga, wgb, wpa, wpb, wo, gf, wup, ffn_conv_w[0], row(ffn_conv_b[0]), wd, gfin)

    xp2 = x_prompt.reshape(nbp * seq, d)
    q_p, qi_p, kb_p, ktf_p, vtf_p, vt_p, kiwi_p, ki2_p = _qkv_call(
        xp2, gm, wbig_p, wsm, wkvt, *_rope_tables(jnp.arange(seq, dtype=I32)), tm=512, seq=seq, prompt=True)
    wit_p = jnp.swapaxes(kiwi_p[:, IDX_DIM:IDX_DIM + IDX_HEADS].reshape(nbp, seq, IDX_HEADS), 1, 2)
    sh3 = lambda t, w: t.reshape(nbp, seq, w)
    attn_p = _attn_prompt_call(sh3(qi_p, a), wit_p, sh3(ki2_p, LANES), sh3(q_p, a), sh3(kb_p, a), vt_p, tq=256)
    yb_p, lconv_p, lh_p = _lru_prompt_call(x_prompt, gm, wxl, wgl, *lru_args, tc=512)
    y_p, fconv_p = _ffn_call(xp2, attn_p.reshape(nbp * seq, a), yb_p.reshape(nbp * seq, LRU_WIDTH), *ffn_w,
                             jnp.zeros((8, LANES), F32), tm=512, fc=d_ff, seq=seq, sample_nb=0)

    ns = nbs * nts
    xs2 = jnp.swapaxes(x_sample, 0, 1).reshape(ns, d)
    pos_s = jnp.repeat(past + jnp.arange(nts, dtype=I32), nbs)
    q_s, qi_s, kf_s, vf_s, kiwi_s = _qkv_call(xs2, gm, wbig, wsm, wkvt, *_rope_tables(pos_s),
                                              tm=ns, seq=nts, prompt=False)
    bm = lambda t, w: jnp.swapaxes(t.reshape(nts, nbs, w), 0, 1)
    k_s4 = bm(kf_s, a).reshape(nbs, nts, N_HEADS, HEAD_DIM)
    v_s4 = bm(vf_s, a).reshape(nbs, nts, N_HEADS, HEAD_DIM)
    ki_s = bm(kiwi_s[:, :IDX_DIM], IDX_DIM)
    wi_s = bm(kiwi_s[:, IDX_DIM:IDX_DIM + IDX_HEADS], IDX_HEADS)
    qi_hq = jnp.swapaxes(bm(qi_s, a).reshape(nbs, nts, IDX_HEADS, IDX_DIM), 1, 2).reshape(
        nbs, IDX_HEADS * nts, IDX_DIM)
    wi_hq = jnp.broadcast_to(jnp.swapaxes(wi_s, 1, 2).reshape(nbs, IDX_HEADS * nts, 1),
                             (nbs, IDX_HEADS * nts, LANES))
    kinew = jnp.pad(ki_s.astype(BF16), ((0, 0), (0, LANES - nts), (0, 0)))
    kidx_t = jnp.transpose(cache_kidx, (0, 1, 3, 2))
    ck_t = jnp.transpose(cache_k, (0, 1, 3, 4, 2))
    cv_t = jnp.transpose(cache_v, (0, 1, 3, 4, 2))
    bias_s = _select_call(page_table, qi_hq, wi_hq, kinew, kidx_t, nt=nts)
    bias_t = jnp.pad(jnp.swapaxes(bias_s, 1, 2), ((0, 0), (0, 0), (0, S_ROWS - nts), (0, 0)))
    qh = jnp.swapaxes(bm(q_s, a).reshape(nbs, nts, N_HEADS, HEAD_DIM), 1, 2)
    qh = jnp.pad(qh, ((0, 0), (0, 0), (0, Q_ROWS - nts), (0, 0))).astype(BF16)
    new_t = lambda t: jnp.pad(jnp.transpose(t, (0, 2, 3, 1)), ((0, 0), (0, 0), (0, 0), (0, LANES - nts)))
    o_s = _dense_sample_call(page_table, qh, bias_t, new_t(k_s4), new_t(v_s4), ck_t, cv_t, gp=16)
    attn_s = jnp.transpose(o_s[:, :, :nts], (2, 0, 1, 3)).reshape(ns, a).astype(BF16)
    tmaj = lambda s: jnp.swapaxes(s, 0, 1).reshape(-1, s.shape[-1])
    yb_s, lconv_s, lh_s = _lru_sample_call(xs2, gm, wxl, wgl, *lru_args, tmaj(state_lru_conv[0]),
                                           state_lru_h[0], nb=nbs, nt=nts)
    y_s, fconv_s = _ffn_call(xs2, attn_s, yb_s, *ffn_w, tmaj(state_ffn_conv[0]),
                             tm=ns, fc=1024, seq=nts, sample_nb=nbs)
    bmaj = lambda t, r: jnp.swapaxes(t.reshape(r, nbs, t.shape[-1]), 0, 1)

    return (y_p.reshape(nbp, seq, d),
            bmaj(y_s, nts),
            jnp.transpose(ktf_p.reshape(nbp, N_HEADS, HEAD_DIM, seq), (0, 3, 1, 2))[None],
            jnp.transpose(vtf_p.reshape(nbp, N_HEADS, HEAD_DIM, seq), (0, 3, 1, 2))[None],
            kiwi_p[:, :IDX_DIM].reshape(1, nbp, seq, IDX_DIM),
            lconv_p[None],
            lh_p.reshape(1, nbp, LRU_WIDTH),
            fconv_p.reshape(nbp, -1, 8, d_ff)[:, -1, 8 - (FFN_CONV_W - 1):][None],
            k_s4[None],
            v_s4[None],
            ki_s[None],
            bmaj(lconv_s, LRU_CONV_W - 1)[None],
            lh_s[None],
            bmaj(fconv_s, FFN_CONV_W - 1)[None])
```

```python
import functools

import numpy as np
import jax
import jax.numpy as jnp
from jax import lax
from jax.experimental import pallas as pl
from jax.experimental.pallas import tpu as pltpu

F32 = jnp.float32
BF16 = jnp.bfloat16
I32 = jnp.int32

N_HEADS = 8
HEAD_DIM = 64
ATTN_WIDTH = N_HEADS * HEAD_DIM
IDX_HEADS = 8
IDX_DIM = 64
TOPK = 256
LRU_WIDTH = 512
LRU_BLOCKS = 8
LRU_CONV_W = 4
LRU_C = 8.0
FFN_CONV_W = 3
ROPE_THETA = 10000.0
EPS = 1e-6
PAGE_SIZE = 128

LANES = 128
INT_MIN = -2 ** 31
NEG_BIG = -1e30
NEG_INF = float("-inf")
KEY_LOWEST = INT_MIN + 2 ** 23
VMEM_LIMIT = 56 * 1024 * 1024

NT_DIMS = (((1,), (1,)), ((), ()))


def _cparams(n_axes):
    return pltpu.CompilerParams(dimension_semantics=("arbitrary",) * n_axes,
                                vmem_limit_bytes=VMEM_LIMIT)


def _rms(x, g):
    r = lax.rsqrt(jnp.mean(x * x, axis=-1, keepdims=True) + EPS)
    return x * r * g


def _gelu(x):
    c = np.float32(np.sqrt(2.0 / np.pi))
    return x * (0.5 * (1.0 + jnp.tanh(c * (x + np.float32(0.044715) * (x * x * x)))))


def _sigmoid(x):
    return 1.0 / (1.0 + jnp.exp(-x))


def _softplus(z):
    return jnp.maximum(z, 0.0) + jnp.log(1.0 + jnp.exp(-jnp.abs(z)))


def _key_to_f32(k):
    return pltpu.bitcast(k ^ ((k >> 31) & jnp.int32(0x7FFFFFFF)), F32)


def _exact_threshold(count_ge, shape, two_bits=False):
    def enough(trial):
        return count_ge(_key_to_f32(trial)) >= float(TOPK)

    def bisect(i, k):
        trial = k + lax.shift_left(jnp.int32(1), jnp.int32(31) - i)
        return jnp.where(enough(trial), trial, k)

    def bisect2(i, k):
        d2 = lax.shift_left(jnp.int32(1), jnp.int32(31) - 2 * i)
        d1 = lax.shift_left(jnp.int32(1), jnp.int32(30) - 2 * i)
        e1, e2, e3 = enough(k + d1), enough(k + d2), enough(k + d2 + d1)
        return k + jnp.where(e2, jnp.where(e3, d2 + d1, d2), jnp.where(e1, d1, 0))

    k0 = jnp.full(shape, INT_MIN, I32)
    k = lax.fori_loop(0, 16, bisect2, k0) if two_bits else lax.fori_loop(0, 32, bisect, k0)
    k = jnp.maximum(k, jnp.int32(KEY_LOWEST))
    return _key_to_f32(k), _key_to_f32(k + 1)


def _qkv_kernel(x_ref, g_ref, wbig_ref, wsm_ref, wkvt_ref, cos_ref, sa_ref, sb_ref, cost_ref, sint_ref,
                *out_refs, prompt):
    if prompt:
        q_ref, qi_ref, kb_ref, ktf_ref, vtf_ref, vt_ref, kiwi_ref, ki2_ref = out_refs
    else:
        q_ref, qi_ref, kf_ref, vf_ref, kiwi_ref = out_refs
    h = _rms(x_ref[...], g_ref[...]).astype(BF16)
    y = jnp.dot(h, wbig_ref[...], preferred_element_type=F32)
    ys = jnp.dot(h, wsm_ref[...], preferred_element_type=F32)
    cos, sa, sb = cos_ref[...], sa_ref[...], sb_ref[...]
    o_qi = (2 if prompt else 3) * ATTN_WIDTH

    def rope(t):
        return t * cos + pltpu.roll(t, 96, 1) * sa + pltpu.roll(t, 32, 1) * sb

    for j in range(ATTN_WIDTH // LANES):
        sl = slice(LANES * j, LANES * (j + 1))
        qj = rope(y[:, LANES * j:LANES * (j + 1)]) * 0.125
        q_ref[:, sl] = qj.astype(q_ref.dtype)
        kj = rope(y[:, ATTN_WIDTH + LANES * j:ATTN_WIDTH + LANES * (j + 1)])
        if prompt:
            kb_ref[:, sl] = kj.astype(BF16)
        else:
            kf_ref[:, sl] = kj
        qij = rope(y[:, o_qi + LANES * j:o_qi + LANES * (j + 1)]) * 0.125
        qi_ref[:, sl] = qij.astype(BF16)
    ysr = rope(ys)
    lane = lax.broadcasted_iota(I32, ys.shape, 1)
    kiwi_ref[...] = jnp.where(lane < IDX_DIM, ysr, ys * np.float32(IDX_HEADS ** -0.5))
    if not prompt:
        vf_ref[...] = y[:, 2 * ATTN_WIDTH:3 * ATTN_WIDTH]
        return
    ki2_ref[...] = jnp.where(lane < IDX_DIM, ysr, pltpu.roll(ysr, IDX_DIM, 1)).astype(BF16)
    kvt = lax.dot_general(wkvt_ref[...], h, NT_DIMS, preferred_element_type=F32)
    cos_t, sin_t = cost_ref[...], sint_ref[...]
    hh = HEAD_DIM // 2
    for hd in range(N_HEADS):
        x1 = kvt[HEAD_DIM * hd:HEAD_DIM * hd + hh]
        x2 = kvt[HEAD_DIM * hd + hh:HEAD_DIM * (hd + 1)]
        ktf_ref[0, HEAD_DIM * hd:HEAD_DIM * hd + hh, :] = x1 * cos_t - x2 * sin_t
        ktf_ref[0, HEAD_DIM * hd + hh:HEAD_DIM * (hd + 1), :] = x2 * cos_t + x1 * sin_t
    v_t = kvt[ATTN_WIDTH:2 * ATTN_WIDTH]
    vtf_ref[0] = v_t
    vt_ref[0] = v_t.astype(BF16)


def _qkv_call(x2d, g, wbig, wsm, wkvt, cos, sa, sb, cos_t, sin_t, *, tm, seq, prompt):
    n, d = x2d.shape
    nt = n // tm
    tps = seq // tm if prompt else 1
    nb = n // seq if prompt else 1
    tok = lambda w: pl.BlockSpec((tm, w), lambda i: (i, 0))
    const = lambda a: pl.BlockSpec(a.shape, lambda i: (0,) * a.ndim)
    tab = pl.BlockSpec((tm, LANES), lambda i: (i % tps, 0))
    tab_t = pl.BlockSpec((HEAD_DIM // 2, tm), lambda i: (0, i % tps))
    if prompt:
        seq_t = pl.BlockSpec((1, ATTN_WIDTH, tm), lambda i: (i // tps, 0, i % tps))
        out_shape = (jax.ShapeDtypeStruct((n, ATTN_WIDTH), BF16),
                     jax.ShapeDtypeStruct((n, ATTN_WIDTH), BF16),
                     jax.ShapeDtypeStruct((n, ATTN_WIDTH), BF16),
                     jax.ShapeDtypeStruct((nb, ATTN_WIDTH, seq), F32),
                     jax.ShapeDtypeStruct((nb, ATTN_WIDTH, seq), F32),
                     jax.ShapeDtypeStruct((nb, ATTN_WIDTH, seq), BF16),
                     jax.ShapeDtypeStruct((n, LANES), F32),
                     jax.ShapeDtypeStruct((n, LANES), BF16))
        out_specs = (tok(ATTN_WIDTH), tok(ATTN_WIDTH), tok(ATTN_WIDTH), seq_t, seq_t, seq_t,
                     tok(LANES), tok(LANES))
    else:
        out_shape = (jax.ShapeDtypeStruct((n, ATTN_WIDTH), F32),
                     jax.ShapeDtypeStruct((n, ATTN_WIDTH), BF16),
                     jax.ShapeDtypeStruct((n, ATTN_WIDTH), F32),
                     jax.ShapeDtypeStruct((n, ATTN_WIDTH), F32),
                     jax.ShapeDtypeStruct((n, LANES), F32))
        out_specs = (tok(ATTN_WIDTH), tok(ATTN_WIDTH), tok(ATTN_WIDTH), tok(ATTN_WIDTH), tok(LANES))
    return pl.pallas_call(
        functools.partial(_qkv_kernel, prompt=prompt),
        grid=(nt,),
        in_specs=[tok(d), const(g), const(wbig), const(wsm), const(wkvt), tab, tab, tab, tab_t, tab_t],
        out_specs=out_specs, out_shape=out_shape,
        compiler_params=_cparams(1),
        name="qkv_prompt" if prompt else "qkv_sample",
    )(x2d, g, wbig, wsm, wkvt, cos, sa, sb, cos_t, sin_t)


def _lru_gates(xc, wa_ref, ba_ref, wx_ref, bx_ref, lam_ref):
    xcb = xc.astype(BF16)
    r = _sigmoid(jnp.dot(xcb, wa_ref[...], preferred_element_type=F32) + ba_ref[...])
    i = _sigmoid(jnp.dot(xcb, wx_ref[...], preferred_element_type=F32) + bx_ref[...])
    log_a = -LRU_C * r * _softplus(-lam_ref[...])
    a = jnp.exp(log_a)
    mult = jnp.sqrt(1.0 - jnp.exp(2.0 * log_a))
    return a, mult * (i * xc)


def _lru_prompt_kernel(x_ref, g_ref, wxl_ref, wgl_ref, cw_ref, cb_ref, wa_ref, ba_ref, wx_ref, bx_ref,
                       lam_ref, y_ref, conv_ref, hlast_ref, xs_ref, a_ref, b_ref, hs_ref, hc_ref, *, tc):
    t = pl.program_id(1)

    @pl.when(t == 0)
    def _():
        xs_ref[0:8, :] = jnp.zeros((8, LRU_WIDTH), F32)
        hc_ref[...] = jnp.zeros(hc_ref.shape, F32)

    h = _rms(x_ref[0], g_ref[...]).astype(BF16)
    xl = jnp.dot(h, wxl_ref[...], preferred_element_type=F32)
    gl = jnp.dot(h, wgl_ref[...], preferred_element_type=F32)
    xs_ref[8:8 + tc, :] = xl
    cw = cw_ref[...]
    xc = cb_ref[...] + xs_ref[5:5 + tc, :] * cw[0:1]
    xc = xc + xs_ref[6:6 + tc, :] * cw[1:2]
    xc = xc + xs_ref[7:7 + tc, :] * cw[2:3]
    xc = xc + xl * cw[3:4]
    tail = xl[tc - 8:tc, :]
    xs_ref[0:8, :] = tail
    conv_ref[0] = tail[8 - (LRU_CONV_W - 1):8, :]

    a, bt = _lru_gates(xc, wa_ref, ba_ref, wx_ref, bx_ref, lam_ref)
    a_ref[...] = a
    b_ref[...] = bt

    def step(i, hp):
        hn = a_ref[pl.ds(i, 1), :] * hp + b_ref[pl.ds(i, 1), :]
        hs_ref[pl.ds(i, 1), :] = hn
        return hn

    hl = lax.fori_loop(0, tc, step, hc_ref[...], unroll=8)
    hc_ref[...] = hl
    hlast_ref[0] = hl
    y_ref[0] = (hs_ref[...] * _gelu(gl)).astype(BF16)


def _lru_prompt_call(x3d, g, wxl, wgl, cw, cb, wa, ba, wx, bx, lam, *, tc):
    nb, seq, d = x3d.shape
    const = lambda a: pl.BlockSpec(a.shape, lambda b, t: (0,) * a.ndim)
    return pl.pallas_call(
        functools.partial(_lru_prompt_kernel, tc=tc),
        grid=(nb, seq // tc),
        in_specs=[pl.BlockSpec((1, tc, d), lambda b, t: (b, t, 0))] +
                 [const(a) for a in (g, wxl, wgl, cw, cb, wa, ba, wx, bx, lam)],
        out_specs=(pl.BlockSpec((1, tc, LRU_WIDTH), lambda b, t: (b, t, 0)),
                   pl.BlockSpec((1, LRU_CONV_W - 1, LRU_WIDTH), lambda b, t: (b, 0, 0)),
                   pl.BlockSpec((1, 1, LRU_WIDTH), lambda b, t: (b, 0, 0))),
        out_shape=(jax.ShapeDtypeStruct((nb, seq, LRU_WIDTH), BF16),
                   jax.ShapeDtypeStruct((nb, LRU_CONV_W - 1, LRU_WIDTH), F32),
                   jax.ShapeDtypeStruct((nb, 1, LRU_WIDTH), F32)),
        scratch_shapes=[pltpu.VMEM((tc + 8, LRU_WIDTH), F32), pltpu.VMEM((tc, LRU_WIDTH), F32),
                        pltpu.VMEM((tc, LRU_WIDTH), F32), pltpu.VMEM((tc, LRU_WIDTH), F32),
                        pltpu.VMEM((1, LRU_WIDTH), F32)],
        compiler_params=_cparams(2),
        name="lru_prompt",
    )(x3d, g, wxl, wgl, cw, cb, wa, ba, wx, bx, lam)


def _lru_sample_kernel(x_ref, g_ref, wxl_ref, wgl_ref, cw_ref, cb_ref, wa_ref, ba_ref, wx_ref, bx_ref,
                       lam_ref, buf_ref, h0_ref, y_ref, conv_ref, hlast_ref, *, nb, nt):
    h = _rms(x_ref[...], g_ref[...]).astype(BF16)
    xl = jnp.dot(h, wxl_ref[...], preferred_element_type=F32)
    gl = jnp.dot(h, wgl_ref[...], preferred_element_type=F32)
    xx = jnp.concatenate([buf_ref[...], xl], axis=0)
    cw = cw_ref[...]
    n = nb * nt
    xc = cb_ref[...] + xx[0:n] * cw[0:1]
    for j in range(1, LRU_CONV_W):
        xc = xc + xx[j * nb:j * nb + n] * cw[j:j + 1]
    conv_ref[...] = xx[n:n + (LRU_CONV_W - 1) * nb]
    a, bt = _lru_gates(xc, wa_ref, ba_ref, wx_ref, bx_ref, lam_ref)
    hp = h0_ref[...]
    hs = []
    for t in range(nt):
        hp = a[t * nb:(t + 1) * nb] * hp + bt[t * nb:(t + 1) * nb]
        hs.append(hp)
    hlast_ref[...] = hp
    y_ref[...] = (jnp.concatenate(hs, axis=0) * _gelu(gl)).astype(BF16)


def _lru_sample_call(x2d, g, wxl, wgl, cw, cb, wa, ba, wx, bx, lam, buf, h0, *, nb, nt):
    n = nb * nt
    return pl.pallas_call(
        functools.partial(_lru_sample_kernel, nb=nb, nt=nt),
        out_shape=(jax.ShapeDtypeStruct((n, LRU_WIDTH), BF16),
                   jax.ShapeDtypeStruct(((LRU_CONV_W - 1) * nb, LRU_WIDTH), F32),
                   jax.ShapeDtypeStruct((nb, LRU_WIDTH), F32)),
        compiler_params=pltpu.CompilerParams(vmem_limit_bytes=VMEM_LIMIT),
        name="lru_sample",
    )(x2d, g, wxl, wgl, cw, cb, wa, ba, wx, bx, lam, buf, h0)


def _attn_prompt_kernel(qi_ref, wit_ref, ki2_ref, q_ref, k_ref, vt_ref, o_ref, sc_ref, bias_ref, *, tq, tk, seq):
    qb = pl.program_id(1)
    nk = qb + 1
    half = lax.broadcasted_iota(I32, (tq, LANES), 1) // HEAD_DIM
    kpos0 = lax.broadcasted_iota(I32, (tk, tq), 0)
    qpos = qb * tq + lax.broadcasted_iota(I32, (tk, tq), 1)

    def masked_pair(ref, h):
        pair = ref[0, :, LANES * (h // 2):LANES * (h // 2 + 1)]
        return jnp.where(half == (h % 2), pair, jnp.zeros_like(pair))

    qim = [masked_pair(qi_ref, h) for h in range(IDX_HEADS)]
    wit = wit_ref[0]

    def score_chunk(c, carry):
        off = pl.multiple_of(c * tk, tk)
        kc = ki2_ref[0, pl.ds(off, tk), :]
        acc = jnp.zeros((tk, tq), F32)
        for h in range(IDX_HEADS):
            s = lax.dot_general(kc, qim[h], NT_DIMS, preferred_element_type=F32)
            acc = acc + jnp.maximum(s, 0.0) * wit[h:h + 1, :]
        sc_ref[pl.ds(off, tk), :] = jnp.where(kpos0 + off <= qpos, acc, NEG_INF)
        return carry

    lax.fori_loop(0, nk, score_chunk, 0)

    def total(weight):
        def body(c, part):
            off = pl.multiple_of(c * tk, tk)
            w = weight(sc_ref[pl.ds(off, tk), :], off)
            return part + jnp.sum(w.reshape(tk // 8, 8, tq), axis=0)
        part = lax.fori_loop(0, nk, body, jnp.zeros((8, tq), F32))
        return jnp.sum(part, axis=0, keepdims=True)

    def count_ge(trial):
        return total(lambda sc, off: jnp.where(sc >= trial, 1.0, 0.0))

    t, t_next = _exact_threshold(count_ge, (1, tq))
    pos_bits = seq.bit_length()

    def tie_limit():
        need = float(TOPK) - count_ge(t_next)

        def ties_below(jt):
            def weight(sc, off):
                x = jnp.where(kpos0 + off < jt, sc, NEG_INF)
                return jnp.where(x >= t, 1.0, 0.0) - jnp.where(x >= t_next, 1.0, 0.0)
            return total(weight)

        def bis(i, j):
            jt = j + lax.shift_left(jnp.int32(1), jnp.int32(pos_bits - 1) - i)
            return jnp.where(ties_below(jt) <= need, jt, j)

        return lax.fori_loop(0, pos_bits, bis, jnp.zeros((1, tq), I32))

    jlim = lax.cond(jnp.max(count_ge(t)) > float(TOPK), tie_limit,
                    lambda: jnp.full((1, tq), 2 ** pos_bits - 1, I32))

    def write_bias(c, carry):
        off = pl.multiple_of(c * tk, tk)
        thr = jnp.where(kpos0 + off < jlim, t, t_next)
        bias_ref[pl.ds(off, tk), :] = jnp.where(sc_ref[pl.ds(off, tk), :] >= thr, 0.0, NEG_BIG)
        return carry

    lax.fori_loop(0, nk, write_bias, 0)

    qm = [masked_pair(q_ref, h) for h in range(N_HEADS)]

    def chunk(c, carry):
        off = pl.multiple_of(c * tk, tk)
        bias = bias_ref[pl.ds(off, tk), :]
        ss = []
        for h in range(N_HEADS):
            kc = k_ref[0, pl.ds(off, tk), LANES * (h // 2):LANES * (h // 2 + 1)]
            ss.append(lax.dot_general(kc, qm[h], NT_DIMS, preferred_element_type=F32))
        ps, stats = [], []
        for h in range(N_HEADS):
            m, l = carry[3 * h:3 * h + 2]
            s = ss[h] + bias
            m_new = jnp.maximum(m, jnp.max(s, axis=0, keepdims=True))
            p = jnp.exp(s - m_new)
            alpha = jnp.exp(m - m_new)
            stats.append((m_new, alpha * l + jnp.sum(p, axis=0, keepdims=True), alpha))
            ps.append(p.astype(BF16))
        new = []
        for h in range(N_HEADS):
            vt = vt_ref[0, HEAD_DIM * h:HEAD_DIM * (h + 1), pl.ds(off, tk)]
            m_new, l_new, alpha = stats[h]
            acc_new = alpha * carry[3 * h + 2] + jnp.dot(vt, ps[h], preferred_element_type=F32)
            new += [m_new, l_new, acc_new]
        return tuple(new)

    init = (jnp.full((1, tq), NEG_BIG, F32), jnp.zeros((1, tq), F32),
            jnp.zeros((HEAD_DIM, tq), F32)) * N_HEADS
    res = lax.fori_loop(0, nk, chunk, init)
    outs = [res[3 * h + 2] / res[3 * h + 1] for h in range(N_HEADS)]
    o_t = jnp.concatenate(outs, axis=0).astype(BF16)
    eye = (lax.broadcasted_iota(I32, (tq, tq), 0) == lax.broadcasted_iota(I32, (tq, tq), 1))
    eye = jnp.where(eye, 1.0, 0.0).astype(BF16)
    o_ref[0] = lax.dot_general(eye, o_t, NT_DIMS, preferred_element_type=F32).astype(BF16)


def _attn_prompt_call(qi, wit, ki2, q, k, vt, *, tq):
    nb, seq, _ = q.shape
    return pl.pallas_call(
        functools.partial(_attn_prompt_kernel, tq=tq, tk=tq, seq=seq),
        grid=(nb, seq // tq),
        in_specs=[pl.BlockSpec((1, tq, ATTN_WIDTH), lambda b, i: (b, i, 0)),
                  pl.BlockSpec((1, IDX_HEADS, tq), lambda b, i: (b, 0, i)),
                  pl.BlockSpec((1, seq, LANES), lambda b, i: (b, 0, 0)),
                  pl.BlockSpec((1, tq, ATTN_WIDTH), lambda b, i: (b, i, 0)),
                  pl.BlockSpec((1, seq, ATTN_WIDTH), lambda b, i: (b, 0, 0)),
                  pl.BlockSpec((1, ATTN_WIDTH, seq), lambda b, i: (b, 0, 0))],
        out_specs=pl.BlockSpec((1, tq, ATTN_WIDTH), lambda b, i: (b, i, 0)),
        out_shape=jax.ShapeDtypeStruct((nb, seq, ATTN_WIDTH), BF16),
        scratch_shapes=[pltpu.VMEM((seq, tq), F32), pltpu.VMEM((seq, tq), F32)],
        compiler_params=_cparams(2),
        name="attn_prompt",
    )(qi, wit, ki2, q, k, vt)


def _chunk_rows(n_pages):
    return -(-(n_pages + 1) // 8) * 8


def _select_kernel(pt_ref, qi_ref, wi_ref, kinew_ref, kidx_hbm, bias_ref, kbuf, sem, sc_ref, *,
                   n_pages, nt, group):
    b = pl.program_id(0)
    nb = pl.num_programs(0)
    slot = b % 2
    n_rows_sc = _chunk_rows(n_pages)

    def page_copy(bb, sl, p):
        return pltpu.make_async_copy(kidx_hbm.at[0, pt_ref[bb, p]], kbuf.at[sl, p], sem.at[sl])

    def fetch(bb, sl):
        def body(p, carry):
            page_copy(bb, sl, p).start()
            return carry
        lax.fori_loop(0, n_pages, body, 0, unroll=8)

    @pl.when(b == 0)
    def _():
        fetch(0, 0)

    @pl.when(b + 1 < nb)
    def _():
        fetch(b + 1, 1 - slot)

    pltpu.make_async_copy(kidx_hbm.at[0, pl.ds(0, n_pages)], kbuf.at[slot], sem.at[slot]).wait()

    qi = qi_ref[0]
    wi = wi_ref[0]
    n_rows = IDX_HEADS * nt

    def head_sum(s):
        e = s[0:8]
        for r in range(1, n_rows // 8):
            e = e + s[8 * r:8 * r + 8]
        return e[0:nt] + e[nt:2 * nt]

    wi_g = jnp.concatenate([wi] * group, axis=1)

    def score_group(gi, carry):
        row0 = gi * group
        kc = jnp.concatenate([kbuf[slot, row0 + j] for j in range(group)], axis=1).astype(BF16)
        s = jnp.dot(qi, kc, preferred_element_type=F32)
        sc = head_sum(jnp.maximum(s, 0.0) * wi_g)
        for q in range(nt):
            for j in range(group):
                sc_ref[q, pl.ds(row0 + j, 1), :] = sc[q:q + 1, LANES * j:LANES * (j + 1)]
        return carry

    lax.fori_loop(0, n_pages // group, score_group, 0)

    s_new = lax.dot_general(qi, kinew_ref[0], NT_DIMS, preferred_element_type=F32)
    sc_new = head_sum(jnp.maximum(s_new, 0.0) * wi)
    lane = lax.broadcasted_iota(I32, (nt, LANES), 1)
    qrow = lax.broadcasted_iota(I32, (nt, LANES), 0)
    sc_new = jnp.where(lane <= qrow, sc_new, NEG_INF)
    for q in range(nt):
        sc_ref[q, n_pages:n_pages + 1, :] = sc_new[q:q + 1, :]
        sc_ref[q, n_pages + 1:n_rows_sc, :] = jnp.full((n_rows_sc - n_pages - 1, LANES), NEG_INF, F32)

    def reduce3(x, op):
        return op(op(x, axis=1, keepdims=True), axis=2, keepdims=True)

    def count_ge(trial, limit=None):
        sc = sc_ref[...]
        if limit is not None:
            sc = jnp.where(pos < limit, sc, NEG_INF)
        return reduce3(jnp.where(sc >= trial, 1.0, 0.0), jnp.sum)

    scores = sc_ref[...]
    pos = (lax.broadcasted_iota(I32, scores.shape, 1) * LANES + lax.broadcasted_iota(I32, scores.shape, 2))
    t, t_next = _exact_threshold(count_ge, (nt, 1, 1), two_bits=True)

    pos_bits = (n_rows_sc * LANES).bit_length()

    def tie_limit():
        need = float(TOPK) - count_ge(t_next)

        def bis(i, j):
            jt = j + lax.shift_left(jnp.int32(1), jnp.int32(pos_bits - 1) - i)
            return jnp.where(count_ge(t, jt) - count_ge(t_next, jt) <= need, jt, j)

        return lax.fori_loop(0, pos_bits, bis, jnp.zeros((nt, 1, 1), I32))

    jlim = lax.cond(jnp.max(count_ge(t)) > float(TOPK), tie_limit,
                    lambda: jnp.full((nt, 1, 1), 2 ** pos_bits - 1, I32))
    thr = jnp.where(pos < jlim, t, t_next)
    bias_ref[0] = jnp.where(scores >= thr, 0.0, NEG_BIG)


def _select_call(page_table, qi_s, wi_s, kinew, kidx_t, *, nt):
    nb, n_pages = page_table.shape
    rows = _chunk_rows(n_pages)
    grid_spec = pltpu.PrefetchScalarGridSpec(
        num_scalar_prefetch=1,
        grid=(nb,),
        in_specs=[pl.BlockSpec((1, IDX_HEADS * nt, IDX_DIM), lambda b, pt: (b, 0, 0)),
                  pl.BlockSpec((1, IDX_HEADS * nt, LANES), lambda b, pt: (b, 0, 0)),
                  pl.BlockSpec((1, LANES, IDX_DIM), lambda b, pt: (b, 0, 0)),
                  pl.BlockSpec(memory_space=pl.ANY)],
        out_specs=pl.BlockSpec((1, nt, rows, LANES), lambda b, pt: (b, 0, 0, 0)),
        scratch_shapes=[pltpu.VMEM((2, n_pages, IDX_DIM, PAGE_SIZE), F32),
                        pltpu.SemaphoreType.DMA((2,)),
                        pltpu.VMEM((nt, rows, LANES), F32)])
    return pl.pallas_call(
        functools.partial(_select_kernel, n_pages=n_pages, nt=nt, group=16),
        grid_spec=grid_spec,
        out_shape=jax.ShapeDtypeStruct((nb, nt, rows, LANES), F32),
        compiler_params=_cparams(1),
        name="select_sample",
    )(page_table, qi_s, wi_s, kinew, kidx_t)


Q_ROWS = 16
S_ROWS = 8


def _dense_sample_kernel(pt_ref, q_ref, bias_ref, biasn_ref, knew_ref, vnew_ref, ck_hbm, cv_hbm, o_ref,
                         kb, vb, sem, m_ref, l_ref, acc_ref, *, gp):
    b = pl.program_id(0)
    g = pl.program_id(1)
    ng = pl.num_programs(1)
    n_steps = pl.num_programs(0) * ng
    step = b * ng + g
    slot = step % 2

    def page_copies(st, sl, j):
        page = pt_ref[lax.div(st, ng), lax.rem(st, ng) * gp + j]
        return (pltpu.make_async_copy(ck_hbm.at[0, page], kb.at[sl, j], sem.at[0, sl]),
                pltpu.make_async_copy(cv_hbm.at[0, page], vb.at[sl, j], sem.at[1, sl]))

    def fetch(st, sl):
        for j in range(gp):
            for cp in page_copies(st, sl, j):
                cp.start()

    @pl.when(step == 0)
    def _():
        fetch(0, 0)

    @pl.when(step + 1 < n_steps)
    def _():
        fetch(step + 1, 1 - slot)

    for j in range(gp):
        for cp in page_copies(step, slot, j):
            cp.wait()

    @pl.when(g == 0)
    def _():
        m_ref[...] = jnp.full(m_ref.shape, NEG_BIG, F32)
        l_ref[...] = jnp.zeros(l_ref.shape, F32)
        acc_ref[...] = jnp.zeros(acc_ref.shape, F32)

    def attend(k_of, v_of, bias):
        ss = [jnp.dot(q_ref[0, h], k_of(h), preferred_element_type=F32)[0:S_ROWS] for h in range(N_HEADS)]
        ps, alphas = [], []
        for h in range(N_HEADS):
            s = ss[h] + bias
            m_old = m_ref[h]
            m_new = jnp.maximum(m_old, jnp.max(s, axis=1, keepdims=True))
            p = jnp.exp(s - m_new)
            alpha = jnp.exp(m_old - m_new)
            l_ref[h] = alpha * l_ref[h] + jnp.sum(p, axis=1, keepdims=True)
            m_ref[h] = m_new
            ps.append(jnp.concatenate([p, jnp.zeros((Q_ROWS - S_ROWS, p.shape[1]), F32)], axis=0).astype(BF16))
            alphas.append(alpha)
        for h in range(N_HEADS):
            pv = lax.dot_general(ps[h], v_of(h), NT_DIMS, preferred_element_type=F32)[0:S_ROWS]
            acc_ref[h] = alphas[h] * acc_ref[h] + pv

    def cat(buf, h):
        return jnp.concatenate([buf[slot, j, h] for j in range(gp)], axis=1).astype(BF16)

    attend(lambda h: cat(kb, h), lambda h: cat(vb, h),
           jnp.concatenate([bias_ref[0, j] for j in range(gp)], axis=1))

    @pl.when(g == ng - 1)
    def _():
        attend(lambda h: knew_ref[0, h].astype(BF16), lambda h: vnew_ref[0, h].astype(BF16), biasn_ref[0, 0])
        for h in range(N_HEADS):
            o_ref[0, h] = acc_ref[h] / l_ref[h]


def _dense_sample_call(page_table, qh, bias_t, knew_t, vnew_t, ck_t, cv_t, *, gp):
    nb, n_pages = page_table.shape
    grid_spec = pltpu.PrefetchScalarGridSpec(
        num_scalar_prefetch=1,
        grid=(nb, n_pages // gp),
        in_specs=[pl.BlockSpec((1, N_HEADS, Q_ROWS, HEAD_DIM), lambda b, g, pt: (b, 0, 0, 0)),
                  pl.BlockSpec((1, gp, S_ROWS, LANES), lambda b, g, pt: (b, g, 0, 0)),
                  pl.BlockSpec((1, 1, S_ROWS, LANES), lambda b, g, pt: (b, n_pages, 0, 0)),
                  pl.BlockSpec((1, N_HEADS, HEAD_DIM, LANES), lambda b, g, pt: (b, 0, 0, 0)),
                  pl.BlockSpec((1, N_HEADS, HEAD_DIM, LANES), lambda b, g, pt: (b, 0, 0, 0)),
                  pl.BlockSpec(memory_space=pl.ANY),
                  pl.BlockSpec(memory_space=pl.ANY)],
        out_specs=pl.BlockSpec((1, N_HEADS, S_ROWS, HEAD_DIM), lambda b, g, pt: (b, 0, 0, 0)),
        scratch_shapes=[pltpu.VMEM((2, gp, N_HEADS, HEAD_DIM, PAGE_SIZE), F32),
                        pltpu.VMEM((2, gp, N_HEADS, HEAD_DIM, PAGE_SIZE), F32),
                        pltpu.SemaphoreType.DMA((2, 2)),
                        pltpu.VMEM((N_HEADS, S_ROWS, 1), F32),
                        pltpu.VMEM((N_HEADS, S_ROWS, 1), F32),
                        pltpu.VMEM((N_HEADS, S_ROWS, HEAD_DIM), F32)])
    return pl.pallas_call(
        functools.partial(_dense_sample_kernel, gp=gp),
        grid_spec=grid_spec,
        out_shape=jax.ShapeDtypeStruct((nb, N_HEADS, S_ROWS, HEAD_DIM), F32),
        compiler_params=_cparams(2),
        name="dense_sample",
    )(page_table, qh, bias_t, bias_t, knew_t, vnew_t, ck_t, cv_t)


def _ffn_kernel(x_ref, at_ref, yb_ref, gm_ref, wga_ref, wgb_ref, wpa_ref, wpb_ref, wo_ref, gf_ref,
                wua_ref, wub_ref, fcw_ref, fcb_ref, wd_ref, gfin_ref, buf_ref,
                y_ref, st_ref, x1_ref, h2_ref, acc_ref, us_ref, carry_ref, *, tm, fc, tps, sample_nb):
    i = pl.program_id(0)
    c = pl.program_id(1)
    nc = pl.num_programs(1)

    @pl.when(c == 0)
    def _():
        x = x_ref[...]
        h = _rms(x, gm_ref[...]).astype(BF16)
        ga = jnp.dot(h, wga_ref[...], preferred_element_type=F32)
        gb = jnp.dot(h, wgb_ref[...], preferred_element_type=F32)
        ya = jnp.dot(at_ref[...], wpa_ref[...], preferred_element_type=F32)
        yb = jnp.dot(yb_ref[...], wpb_ref[...], preferred_element_type=F32)
        mix = _sigmoid(ga) * ya + _sigmoid(gb) * yb
        x1 = x + jnp.dot(mix.astype(BF16), wo_ref[...], preferred_element_type=F32)
        x1_ref[...] = x1
        h2_ref[...] = _rms(x1, gf_ref[...]).astype(BF16)
        acc_ref[...] = jnp.zeros(acc_ref.shape, F32)

    h2 = h2_ref[...]
    ua = jnp.dot(h2, wua_ref[...], preferred_element_type=F32)
    ub = jnp.dot(h2, wub_ref[...], preferred_element_type=F32)
    w = fcw_ref[...]
    if sample_nb:
        nb = sample_nb
        us = jnp.concatenate([buf_ref[...], ua], axis=0)
        uc = fcb_ref[...] + us[0:tm] * w[0:1]
        for j in range(1, FFN_CONV_W):
            uc = uc + us[j * nb:j * nb + tm] * w[j:j + 1]
        st_ref[...] = us[tm:tm + (FFN_CONV_W - 1) * nb]
    else:
        first = (i % tps) == 0
        us_ref[0:8, :] = jnp.where(first, jnp.zeros((8, fc), F32), carry_ref[c])
        us_ref[8:8 + tm, :] = ua
        uc = fcb_ref[...] + us_ref[6:6 + tm, :] * w[0:1]
        uc = uc + us_ref[7:7 + tm, :] * w[1:2]
        uc = uc + ua * w[2:3]
        tail = ua[tm - 8:tm, :]
        carry_ref[c] = tail
        st_ref[0] = tail
    act = (_gelu(uc) * ub).astype(BF16)
    acc_ref[...] += jnp.dot(act, wd_ref[...], preferred_element_type=F32)

    @pl.when(c == nc - 1)
    def _():
        y_ref[...] = _rms(x1_ref[...] + acc_ref[...], gfin_ref[...])


def _ffn_call(x2d, attn, yb, gm, wga, wgb, wpa, wpb, wo, gf, wup, fcw, fcb, wd, gfin, buf, *,
              tm, fc, seq, sample_nb):
    n, d = x2d.shape
    d_ff = wd.shape[0]
    nc = d_ff // fc
    tps = seq // tm if not sample_nb else 1
    tok = lambda w: pl.BlockSpec((tm, w), lambda i, c: (i, 0))
    const = lambda a: pl.BlockSpec(a.shape, lambda i, c: (0,) * a.ndim, pipeline_mode=pl.Buffered(1))
    wmode = pl.Buffered(1) if nc == 1 else None
    if sample_nb:
        nst = (FFN_CONV_W - 1) * sample_nb
        buf_spec = pl.BlockSpec((nst, fc), lambda i, c: (0, c))
        st_spec = pl.BlockSpec((nst, fc), lambda i, c: (0, c))
        st_shape = jax.ShapeDtypeStruct((nst, d_ff), F32)
    else:
        buf_spec = pl.BlockSpec((8, LANES), lambda i, c: (0, 0))
        st_spec = pl.BlockSpec((1, 8, fc), lambda i, c: (i, 0, c))
        st_shape = jax.ShapeDtypeStruct((n // tm, 8, d_ff), F32)
    return pl.pallas_call(
        functools.partial(_ffn_kernel, tm=tm, fc=fc, tps=tps, sample_nb=sample_nb),
        grid=(n // tm, nc),
        in_specs=[tok(d), tok(ATTN_WIDTH), tok(LRU_WIDTH), const(gm), const(wga), const(wgb), const(wpa),
                  const(wpb), const(wo), const(gf),
                  pl.BlockSpec((d, fc), lambda i, c: (0, c), pipeline_mode=wmode),
                  pl.BlockSpec((d, fc), lambda i, c: (0, nc + c), pipeline_mode=wmode),
                  pl.BlockSpec((FFN_CONV_W, fc), lambda i, c: (0, c)),
                  pl.BlockSpec((1, fc), lambda i, c: (0, c)),
                  pl.BlockSpec((fc, d), lambda i, c: (c, 0), pipeline_mode=wmode),
                  const(gfin), buf_spec],
        out_specs=(tok(d), st_spec),
        out_shape=(jax.ShapeDtypeStruct((n, d), F32), st_shape),
        scratch_shapes=[pltpu.VMEM((tm, d), F32), pltpu.VMEM((tm, d), BF16), pltpu.VMEM((tm, d), F32),
                        pltpu.VMEM((tm + 8, fc), F32), pltpu.VMEM((nc, 8, fc), F32)],
        compiler_params=_cparams(2),
        name="ffn_sample" if sample_nb else "ffn_prompt",
    )(x2d, attn, yb, gm, wga, wgb, wpa, wpb, wo, gf, wup, wup, fcw, fcb, wd, gfin, buf)


def _rope_tables(pos):
    half = HEAD_DIM // 2
    inv = jnp.power(ROPE_THETA, -jnp.arange(half, dtype=F32) / half)
    ang = pos.astype(F32)[:, None] * inv[None, :]
    cos, sin = jnp.cos(ang), jnp.sin(ang)
    z = jnp.zeros_like(sin)
    tile = lambda a, b: jnp.concatenate([a, b, a, b], axis=1)
    return tile(cos, cos), tile(-sin, z), tile(z, sin), cos.T, sin.T


def _block_diag(w):
    nblk, bw, _ = w.shape
    eye = jnp.eye(nblk, dtype=w.dtype)
    return (eye[:, None, :, None] * w[:, :, None, :]).reshape(nblk * bw, nblk * bw)


def kernel(x_prompt, x_sample, cache_k, cache_v, cache_kidx, page_table, state_lru_conv, state_lru_h,
           state_ffn_conv, norm_mix_g, w_in, lru_conv_w, lru_conv_b, lru_wa, lru_ba, lru_wx, lru_bx,
           lru_lambda, w_proj_a, w_proj_b, w_out, norm_ffn_g, w_up, ffn_conv_w, ffn_conv_b, w_down,
           norm_final_g):
    depth = w_in.shape[0]
    assert depth == 1, "single-layer step"
    nbp, seq, d = x_prompt.shape
    nbs, nts, _ = x_sample.shape
    n_pages = page_table.shape[1]
    past = n_pages * PAGE_SIZE
    d_ff = w_down.shape[1]
    a = ATTN_WIDTH
    row = lambda v: v.reshape(1, -1)

    win = w_in[0]
    o_ki = 4 * a
    o_wi = o_ki + IDX_DIM
    o_xl = o_wi + IDX_HEADS
    o_gl = o_xl + LRU_WIDTH
    o_ga = o_gl + LRU_WIDTH
    o_gb = o_ga + d
    wbig = win[:, :4 * a].astype(BF16)
    wbig_p = jnp.concatenate([wbig[:, :2 * a], wbig[:, 3 * a:]], axis=1)
    wsm = jnp.pad(win[:, o_ki:o_xl], ((0, 0), (0, LANES - IDX_DIM - IDX_HEADS))).astype(BF16)
    wkvt = win[:, a:3 * a].T.astype(BF16)
    wxl = win[:, o_xl:o_gl].astype(BF16)
    wgl = win[:, o_gl:o_ga].astype(BF16)
    wga = win[:, o_ga:o_gb].astype(BF16)
    wgb = win[:, o_gb:o_gb + d].astype(BF16)
    wa_bd = _block_diag(lru_wa[0]).astype(BF16)
    wx_bd = _block_diag(lru_wx[0]).astype(BF16)
    wpa = w_proj_a[0].astype(BF16)
    wpb = w_proj_b[0].astype(BF16)
    wo = w_out[0].astype(BF16)
    wup = w_up[0].astype(BF16)
    wd = w_down[0].astype(BF16)
    gm, gf, gfin = row(norm_mix_g[0]), row(norm_ffn_g[0]), row(norm_final_g)
    lru_args = (lru_conv_w[0], row(lru_conv_b[0]), wa_bd, row(lru_ba[0]), wx_bd, row(lru_bx[0]),
                row(lru_lambda[0]))
    ffn_w = (gm, wga, wgb, wpa, wpb, wo, gf, wup, ffn_conv_w[0], row(ffn_conv_b[0]), wd, gfin)

    xp2 = x_prompt.reshape(nbp * seq, d)
    q_p, qi_p, kb_p, ktf_p, vtf_p, vt_p, kiwi_p, ki2_p = _qkv_call(
        xp2, gm, wbig_p, wsm, wkvt, *_rope_tables(jnp.arange(seq, dtype=I32)), tm=1024, seq=seq, prompt=True)
    wit_p = jnp.swapaxes(kiwi_p[:, IDX_DIM:IDX_DIM + IDX_HEADS].reshape(nbp, seq, IDX_HEADS), 1, 2)
    sh3 = lambda t, w: t.reshape(nbp, seq, w)
    attn_p = _attn_prompt_call(sh3(qi_p, a), wit_p, sh3(ki2_p, LANES), sh3(q_p, a), sh3(kb_p, a), vt_p, tq=256)
    yb_p, lconv_p, lh_p = _lru_prompt_call(x_prompt, gm, wxl, wgl, *lru_args, tc=1024)
    y_p, fconv_p = _ffn_call(xp2, attn_p.reshape(nbp * seq, a), yb_p.reshape(nbp * seq, LRU_WIDTH), *ffn_w,
                             jnp.zeros((8, LANES), F32), tm=512, fc=d_ff, seq=seq, sample_nb=0)

    ns = nbs * nts
    xs2 = jnp.swapaxes(x_sample, 0, 1).reshape(ns, d)
    pos_s = jnp.repeat(past + jnp.arange(nts, dtype=I32), nbs)
    q_s, qi_s, kf_s, vf_s, kiwi_s = _qkv_call(xs2, gm, wbig, wsm, wkvt, *_rope_tables(pos_s),
                                              tm=ns, seq=nts, prompt=False)
    bm = lambda t, w: jnp.swapaxes(t.reshape(nts, nbs, w), 0, 1)
    k_s4 = bm(kf_s, a).reshape(nbs, nts, N_HEADS, HEAD_DIM)
    v_s4 = bm(vf_s, a).reshape(nbs, nts, N_HEADS, HEAD_DIM)
    ki_s = bm(kiwi_s[:, :IDX_DIM], IDX_DIM)
    wi_s = bm(kiwi_s[:, IDX_DIM:IDX_DIM + IDX_HEADS], IDX_HEADS)
    qi_hq = jnp.swapaxes(bm(qi_s, a).reshape(nbs, nts, IDX_HEADS, IDX_DIM), 1, 2).reshape(
        nbs, IDX_HEADS * nts, IDX_DIM)
    wi_hq = jnp.broadcast_to(jnp.swapaxes(wi_s, 1, 2).reshape(nbs, IDX_HEADS * nts, 1),
                             (nbs, IDX_HEADS * nts, LANES))
    kinew = jnp.pad(ki_s.astype(BF16), ((0, 0), (0, LANES - nts), (0, 0)))
    kidx_t = jnp.transpose(cache_kidx, (0, 1, 3, 2))
    ck_t = jnp.transpose(cache_k, (0, 1, 3, 4, 2))
    cv_t = jnp.transpose(cache_v, (0, 1, 3, 4, 2))
    bias_s = _select_call(page_table, qi_hq, wi_hq, kinew, kidx_t, nt=nts)
    bias_t = jnp.pad(jnp.swapaxes(bias_s, 1, 2), ((0, 0), (0, 0), (0, S_ROWS - nts), (0, 0)))
    qh = jnp.swapaxes(bm(q_s, a).reshape(nbs, nts, N_HEADS, HEAD_DIM), 1, 2)
    qh = jnp.pad(qh, ((0, 0), (0, 0), (0, Q_ROWS - nts), (0, 0))).astype(BF16)
    new_t = lambda t: jnp.pad(jnp.transpose(t, (0, 2, 3, 1)), ((0, 0), (0, 0), (0, 0), (0, LANES - nts)))
    o_s = _dense_sample_call(page_table, qh, bias_t, new_t(k_s4), new_t(v_s4), ck_t, cv_t, gp=16)
    attn_s = jnp.transpose(o_s[:, :, :nts], (2, 0, 1, 3)).reshape(ns, a).astype(BF16)
    tmaj = lambda s: jnp.swapaxes(s, 0, 1).reshape(-1, s.shape[-1])
    yb_s, lconv_s, lh_s = _lru_sample_call(xs2, gm, wxl, wgl, *lru_args, tmaj(state_lru_conv[0]),
                                           state_lru_h[0], nb=nbs, nt=nts)
    y_s, fconv_s = _ffn_call(xs2, attn_s, yb_s, *ffn_w, tmaj(state_ffn_conv[0]),
                             tm=ns, fc=1024, seq=nts, sample_nb=nbs)
    bmaj = lambda t, r: jnp.swapaxes(t.reshape(r, nbs, t.shape[-1]), 0, 1)

    return (y_p.reshape(nbp, seq, d),
            bmaj(y_s, nts),
            jnp.transpose(ktf_p.reshape(nbp, N_HEADS, HEAD_DIM, seq), (0, 3, 1, 2))[None],
            jnp.transpose(vtf_p.reshape(nbp, N_HEADS, HEAD_DIM, seq), (0, 3, 1, 2))[None],
            kiwi_p[:, :IDX_DIM].reshape(1, nbp, seq, IDX_DIM),
            lconv_p[None],
            lh_p.reshape(1, nbp, LRU_WIDTH),
            fconv_p.reshape(nbp, -1, 8, d_ff)[:, -1, 8 - (FFN_CONV_W - 1):][None],
            k_s4[None],
            v_s4[None],
            ki_s[None],
            bmaj(lconv_s, LRU_CONV_W - 1)[None],
            lh_s[None],
            bmaj(fconv_s, FFN_CONV_W - 1)[None])
```

```python
import functools

import numpy as np
import jax
import jax.numpy as jnp
from jax import lax
from jax.experimental import pallas as pl
from jax.experimental.pallas import tpu as pltpu

F32 = jnp.float32
BF16 = jnp.bfloat16
I32 = jnp.int32

N_HEADS = 8
HEAD_DIM = 64
ATTN_WIDTH = N_HEADS * HEAD_DIM
IDX_HEADS = 8
IDX_DIM = 64
TOPK = 256
LRU_WIDTH = 512
LRU_BLOCKS = 8
LRU_CONV_W = 4
LRU_C = 8.0
FFN_CONV_W = 3
ROPE_THETA = 10000.0
EPS = 1e-6
PAGE_SIZE = 128

LANES = 128
INT_MIN = -2 ** 31
NEG_BIG = -1e30
NEG_INF = float("-inf")
KEY_LOWEST = INT_MIN + 2 ** 23
VMEM_LIMIT = 56 * 1024 * 1024

NT_DIMS = (((1,), (1,)), ((), ()))


def _cparams(n_axes):
    return pltpu.CompilerParams(dimension_semantics=("arbitrary",) * n_axes,
                                vmem_limit_bytes=VMEM_LIMIT)


def _rms(x, g):
    r = lax.rsqrt(jnp.mean(x * x, axis=-1, keepdims=True) + EPS)
    return x * r * g


def _gelu(x):
    c = np.float32(np.sqrt(2.0 / np.pi))
    return x * (0.5 * (1.0 + jnp.tanh(c * (x + np.float32(0.044715) * (x * x * x)))))


def _sigmoid(x):
    return 1.0 / (1.0 + jnp.exp(-x))


def _softplus(z):
    return jnp.maximum(z, 0.0) + jnp.log(1.0 + jnp.exp(-jnp.abs(z)))


def _key_to_f32(k):
    return pltpu.bitcast(k ^ ((k >> 31) & jnp.int32(0x7FFFFFFF)), F32)


def _exact_threshold(count_ge, shape, two_bits=False):
    def enough(trial):
        return count_ge(_key_to_f32(trial)) >= float(TOPK)

    def bisect(i, k):
        trial = k + lax.shift_left(jnp.int32(1), jnp.int32(31) - i)
        return jnp.where(enough(trial), trial, k)

    def bisect2(i, k):
        d2 = lax.shift_left(jnp.int32(1), jnp.int32(31) - 2 * i)
        d1 = lax.shift_left(jnp.int32(1), jnp.int32(30) - 2 * i)
        e1, e2, e3 = enough(k + d1), enough(k + d2), enough(k + d2 + d1)
        return k + jnp.where(e2, jnp.where(e3, d2 + d1, d2), jnp.where(e1, d1, 0))

    k0 = jnp.full(shape, INT_MIN, I32)
    k = lax.fori_loop(0, 16, bisect2, k0) if two_bits else lax.fori_loop(0, 32, bisect, k0)
    k = jnp.maximum(k, jnp.int32(KEY_LOWEST))
    return _key_to_f32(k), _key_to_f32(k + 1)


def _qkv_kernel(x_ref, g_ref, wbig_ref, wsm_ref, wkvt_ref, cos_ref, sa_ref, sb_ref, cost_ref, sint_ref,
                *out_refs, prompt):
    if prompt:
        q_ref, qi_ref, kb_ref, ktf_ref, vtf_ref, vt_ref, kiwi_ref, ki2_ref = out_refs
    else:
        q_ref, qi_ref, kf_ref, vf_ref, kiwi_ref = out_refs
    h = _rms(x_ref[...], g_ref[...]).astype(BF16)
    y = jnp.dot(h, wbig_ref[...], preferred_element_type=F32)
    ys = jnp.dot(h, wsm_ref[...], preferred_element_type=F32)
    cos, sa, sb = cos_ref[...], sa_ref[...], sb_ref[...]
    o_qi = (2 if prompt else 3) * ATTN_WIDTH

    def rope(t):
        return t * cos + pltpu.roll(t, 96, 1) * sa + pltpu.roll(t, 32, 1) * sb

    for j in range(ATTN_WIDTH // LANES):
        sl = slice(LANES * j, LANES * (j + 1))
        qj = rope(y[:, LANES * j:LANES * (j + 1)]) * 0.125
        q_ref[:, sl] = qj.astype(q_ref.dtype)
        kj = rope(y[:, ATTN_WIDTH + LANES * j:ATTN_WIDTH + LANES * (j + 1)])
        if prompt:
            kb_ref[:, sl] = kj.astype(BF16)
        else:
            kf_ref[:, sl] = kj
        qij = rope(y[:, o_qi + LANES * j:o_qi + LANES * (j + 1)]) * 0.125
        qi_ref[:, sl] = qij.astype(BF16)
    ysr = rope(ys)
    lane = lax.broadcasted_iota(I32, ys.shape, 1)
    kiwi_ref[...] = jnp.where(lane < IDX_DIM, ysr, ys * np.float32(IDX_HEADS ** -0.5))
    if not prompt:
        vf_ref[...] = y[:, 2 * ATTN_WIDTH:3 * ATTN_WIDTH]
        return
    ki2_ref[...] = jnp.where(lane < IDX_DIM, ysr, pltpu.roll(ysr, IDX_DIM, 1)).astype(BF16)
    kvt = lax.dot_general(wkvt_ref[...], h, NT_DIMS, preferred_element_type=F32)
    cos_t, sin_t = cost_ref[...], sint_ref[...]
    hh = HEAD_DIM // 2
    for hd in range(N_HEADS):
        x1 = kvt[HEAD_DIM * hd:HEAD_DIM * hd + hh]
        x2 = kvt[HEAD_DIM * hd + hh:HEAD_DIM * (hd + 1)]
        ktf_ref[0, HEAD_DIM * hd:HEAD_DIM * hd + hh, :] = x1 * cos_t - x2 * sin_t
        ktf_ref[0, HEAD_DIM * hd + hh:HEAD_DIM * (hd + 1), :] = x2 * cos_t + x1 * sin_t
    v_t = kvt[ATTN_WIDTH:2 * ATTN_WIDTH]
    vtf_ref[0] = v_t
    vt_ref[0] = v_t.astype(BF16)


def _qkv_call(x2d, g, wbig, wsm, wkvt, cos, sa, sb, cos_t, sin_t, *, tm, seq, prompt):
    n, d = x2d.shape
    nt = n // tm
    tps = seq // tm if prompt else 1
    nb = n // seq if prompt else 1
    tok = lambda w: pl.BlockSpec((tm, w), lambda i: (i, 0))
    const = lambda a: pl.BlockSpec(a.shape, lambda i: (0,) * a.ndim)
    tab = pl.BlockSpec((tm, LANES), lambda i: (i % tps, 0))
    tab_t = pl.BlockSpec((HEAD_DIM // 2, tm), lambda i: (0, i % tps))
    if prompt:
        seq_t = pl.BlockSpec((1, ATTN_WIDTH, tm), lambda i: (i // tps, 0, i % tps))
        out_shape = (jax.ShapeDtypeStruct((n, ATTN_WIDTH), BF16),
                     jax.ShapeDtypeStruct((n, ATTN_WIDTH), BF16),
                     jax.ShapeDtypeStruct((n, ATTN_WIDTH), BF16),
                     jax.ShapeDtypeStruct((nb, ATTN_WIDTH, seq), F32),
                     jax.ShapeDtypeStruct((nb, ATTN_WIDTH, seq), F32),
                     jax.ShapeDtypeStruct((nb, ATTN_WIDTH, seq), BF16),
                     jax.ShapeDtypeStruct((n, LANES), F32),
                     jax.ShapeDtypeStruct((n, LANES), BF16))
        out_specs = (tok(ATTN_WIDTH), tok(ATTN_WIDTH), tok(ATTN_WIDTH), seq_t, seq_t, seq_t,
                     tok(LANES), tok(LANES))
    else:
        out_shape = (jax.ShapeDtypeStruct((n, ATTN_WIDTH), F32),
                     jax.ShapeDtypeStruct((n, ATTN_WIDTH), BF16),
                     jax.ShapeDtypeStruct((n, ATTN_WIDTH), F32),
                     jax.ShapeDtypeStruct((n, ATTN_WIDTH), F32),
                     jax.ShapeDtypeStruct((n, LANES), F32))
        out_specs = (tok(ATTN_WIDTH), tok(ATTN_WIDTH), tok(ATTN_WIDTH), tok(ATTN_WIDTH), tok(LANES))
    return pl.pallas_call(
        functools.partial(_qkv_kernel, prompt=prompt),
        grid=(nt,),
        in_specs=[tok(d), const(g), const(wbig), const(wsm), const(wkvt), tab, tab, tab, tab_t, tab_t],
        out_specs=out_specs, out_shape=out_shape,
        compiler_params=_cparams(1),
        name="qkv_prompt" if prompt else "qkv_sample",
    )(x2d, g, wbig, wsm, wkvt, cos, sa, sb, cos_t, sin_t)


def _lru_gates(xc, wa_ref, ba_ref, wx_ref, bx_ref, lam_ref):
    xcb = xc.astype(BF16)
    r = _sigmoid(jnp.dot(xcb, wa_ref[...], preferred_element_type=F32) + ba_ref[...])
    i = _sigmoid(jnp.dot(xcb, wx_ref[...], preferred_element_type=F32) + bx_ref[...])
    log_a = -LRU_C * r * _softplus(-lam_ref[...])
    a = jnp.exp(log_a)
    mult = jnp.sqrt(1.0 - jnp.exp(2.0 * log_a))
    return a, mult * (i * xc)


def _lru_prompt_kernel(x_ref, g_ref, wxl_ref, wgl_ref, cw_ref, cb_ref, wa_ref, ba_ref, wx_ref, bx_ref,
                       lam_ref, y_ref, conv_ref, hlast_ref, xs_ref, a_ref, b_ref, hs_ref, hc_ref, *, tc):
    t = pl.program_id(1)

    @pl.when(t == 0)
    def _():
        xs_ref[0:8, :] = jnp.zeros((8, LRU_WIDTH), F32)
        hc_ref[...] = jnp.zeros(hc_ref.shape, F32)

    h = _rms(x_ref[0], g_ref[...]).astype(BF16)
    xl = jnp.dot(h, wxl_ref[...], preferred_element_type=F32)
    gl = jnp.dot(h, wgl_ref[...], preferred_element_type=F32)
    xs_ref[8:8 + tc, :] = xl
    cw = cw_ref[...]
    xc = cb_ref[...] + xs_ref[5:5 + tc, :] * cw[0:1]
    xc = xc + xs_ref[6:6 + tc, :] * cw[1:2]
    xc = xc + xs_ref[7:7 + tc, :] * cw[2:3]
    xc = xc + xl * cw[3:4]
    tail = xl[tc - 8:tc, :]
    xs_ref[0:8, :] = tail
    conv_ref[0] = tail[8 - (LRU_CONV_W - 1):8, :]

    a, bt = _lru_gates(xc, wa_ref, ba_ref, wx_ref, bx_ref, lam_ref)
    a_ref[...] = a
    b_ref[...] = bt

    def step(i, hp):
        hn = a_ref[pl.ds(i, 1), :] * hp + b_ref[pl.ds(i, 1), :]
        hs_ref[pl.ds(i, 1), :] = hn
        return hn

    hl = lax.fori_loop(0, tc, step, hc_ref[...], unroll=8)
    hc_ref[...] = hl
    hlast_ref[0] = hl
    y_ref[0] = (hs_ref[...] * _gelu(gl)).astype(BF16)


def _lru_prompt_call(x3d, g, wxl, wgl, cw, cb, wa, ba, wx, bx, lam, *, tc):
    nb, seq, d = x3d.shape
    const = lambda a: pl.BlockSpec(a.shape, lambda b, t: (0,) * a.ndim)
    return pl.pallas_call(
        functools.partial(_lru_prompt_kernel, tc=tc),
        grid=(nb, seq // tc),
        in_specs=[pl.BlockSpec((1, tc, d), lambda b, t: (b, t, 0))] +
                 [const(a) for a in (g, wxl, wgl, cw, cb, wa, ba, wx, bx, lam)],
        out_specs=(pl.BlockSpec((1, tc, LRU_WIDTH), lambda b, t: (b, t, 0)),
                   pl.BlockSpec((1, LRU_CONV_W - 1, LRU_WIDTH), lambda b, t: (b, 0, 0)),
                   pl.BlockSpec((1, 1, LRU_WIDTH), lambda b, t: (b, 0, 0))),
        out_shape=(jax.ShapeDtypeStruct((nb, seq, LRU_WIDTH), BF16),
                   jax.ShapeDtypeStruct((nb, LRU_CONV_W - 1, LRU_WIDTH), F32),
                   jax.ShapeDtypeStruct((nb, 1, LRU_WIDTH), F32)),
        scratch_shapes=[pltpu.VMEM((tc + 8, LRU_WIDTH), F32), pltpu.VMEM((tc, LRU_WIDTH), F32),
                        pltpu.VMEM((tc, LRU_WIDTH), F32), pltpu.VMEM((tc, LRU_WIDTH), F32),
                        pltpu.VMEM((1, LRU_WIDTH), F32)],
        compiler_params=_cparams(2),
        name="lru_prompt",
    )(x3d, g, wxl, wgl, cw, cb, wa, ba, wx, bx, lam)


def _lru_sample_kernel(x_ref, g_ref, wxl_ref, wgl_ref, cw_ref, cb_ref, wa_ref, ba_ref, wx_ref, bx_ref,
                       lam_ref, buf_ref, h0_ref, y_ref, conv_ref, hlast_ref, *, nb, nt):
    h = _rms(x_ref[...], g_ref[...]).astype(BF16)
    xl = jnp.dot(h, wxl_ref[...], preferred_element_type=F32)
    gl = jnp.dot(h, wgl_ref[...], preferred_element_type=F32)
    xx = jnp.concatenate([buf_ref[...], xl], axis=0)
    cw = cw_ref[...]
    n = nb * nt
    xc = cb_ref[...] + xx[0:n] * cw[0:1]
    for j in range(1, LRU_CONV_W):
        xc = xc + xx[j * nb:j * nb + n] * cw[j:j + 1]
    conv_ref[...] = xx[n:n + (LRU_CONV_W - 1) * nb]
    a, bt = _lru_gates(xc, wa_ref, ba_ref, wx_ref, bx_ref, lam_ref)
    hp = h0_ref[...]
    hs = []
    for t in range(nt):
        hp = a[t * nb:(t + 1) * nb] * hp + bt[t * nb:(t + 1) * nb]
        hs.append(hp)
    hlast_ref[...] = hp
    y_ref[...] = (jnp.concatenate(hs, axis=0) * _gelu(gl)).astype(BF16)


def _lru_sample_call(x2d, g, wxl, wgl, cw, cb, wa, ba, wx, bx, lam, buf, h0, *, nb, nt):
    n = nb * nt
    return pl.pallas_call(
        functools.partial(_lru_sample_kernel, nb=nb, nt=nt),
        out_shape=(jax.ShapeDtypeStruct((n, LRU_WIDTH), BF16),
                   jax.ShapeDtypeStruct(((LRU_CONV_W - 1) * nb, LRU_WIDTH), F32),
                   jax.ShapeDtypeStruct((nb, LRU_WIDTH), F32)),
        compiler_params=pltpu.CompilerParams(vmem_limit_bytes=VMEM_LIMIT),
        name="lru_sample",
    )(x2d, g, wxl, wgl, cw, cb, wa, ba, wx, bx, lam, buf, h0)


def _attn_prompt_kernel(qi_ref, wit_ref, ki2_ref, q_ref, k_ref, vt_ref, o_ref, sc_ref, bias_ref, *, tq, tk, seq):
    qb = pl.program_id(1)
    nk = qb + 1
    half = lax.broadcasted_iota(I32, (tq, LANES), 1) // HEAD_DIM
    kpos0 = lax.broadcasted_iota(I32, (tk, tq), 0)
    qpos = qb * tq + lax.broadcasted_iota(I32, (tk, tq), 1)

    def masked_pair(ref, h):
        pair = ref[0, :, LANES * (h // 2):LANES * (h // 2 + 1)]
        return jnp.where(half == (h % 2), pair, jnp.zeros_like(pair))

    qim = [masked_pair(qi_ref, h) for h in range(IDX_HEADS)]
    wit = wit_ref[0]

    def score_chunk(c, carry):
        off = pl.multiple_of(c * tk, tk)
        kc = ki2_ref[0, pl.ds(off, tk), :]
        acc = jnp.zeros((tk, tq), F32)
        for h in range(IDX_HEADS):
            s = lax.dot_general(kc, qim[h], NT_DIMS, preferred_element_type=F32)
            acc = acc + jnp.maximum(s, 0.0) * wit[h:h + 1, :]
        sc_ref[pl.ds(off, tk), :] = jnp.where(kpos0 + off <= qpos, acc, NEG_INF)
        return carry

    lax.fori_loop(0, nk, score_chunk, 0)

    def total(weight):
        def body(c, part):
            off = pl.multiple_of(c * tk, tk)
            w = weight(sc_ref[pl.ds(off, tk), :], off)
            return part + jnp.sum(w.reshape(tk // 8, 8, tq), axis=0)
        part = lax.fori_loop(0, nk, body, jnp.zeros((8, tq), F32))
        return jnp.sum(part, axis=0, keepdims=True)

    def count_ge(trial):
        return total(lambda sc, off: jnp.where(sc >= trial, 1.0, 0.0))

    t, t_next = _exact_threshold(count_ge, (1, tq))
    pos_bits = seq.bit_length()

    def tie_limit():
        need = float(TOPK) - count_ge(t_next)

        def ties_below(jt):
            def weight(sc, off):
                x = jnp.where(kpos0 + off < jt, sc, NEG_INF)
                return jnp.where(x >= t, 1.0, 0.0) - jnp.where(x >= t_next, 1.0, 0.0)
            return total(weight)

        def bis(i, j):
            jt = j + lax.shift_left(jnp.int32(1), jnp.int32(pos_bits - 1) - i)
            return jnp.where(ties_below(jt) <= need, jt, j)

        return lax.fori_loop(0, pos_bits, bis, jnp.zeros((1, tq), I32))

    jlim = lax.cond(jnp.max(count_ge(t)) > float(TOPK), tie_limit,
                    lambda: jnp.full((1, tq), 2 ** pos_bits - 1, I32))

    def write_bias(c, carry):
        off = pl.multiple_of(c * tk, tk)
        thr = jnp.where(kpos0 + off < jlim, t, t_next)
        bias_ref[pl.ds(off, tk), :] = jnp.where(sc_ref[pl.ds(off, tk), :] >= thr, 0.0, NEG_BIG)
        return carry

    lax.fori_loop(0, nk, write_bias, 0)

    qm = [masked_pair(q_ref, h) for h in range(N_HEADS)]

    def chunk(c, carry):
        off = pl.multiple_of(c * tk, tk)
        bias = bias_ref[pl.ds(off, tk), :]
        ss = []
        for h in range(N_HEADS):
            kc = k_ref[0, pl.ds(off, tk), LANES * (h // 2):LANES * (h // 2 + 1)]
            ss.append(lax.dot_general(kc, qm[h], NT_DIMS, preferred_element_type=F32))
        ps, stats = [], []
        for h in range(N_HEADS):
            m, l = carry[3 * h:3 * h + 2]
            s = ss[h] + bias
            m_new = jnp.maximum(m, jnp.max(s, axis=0, keepdims=True))
            p = jnp.exp(s - m_new)
            alpha = jnp.exp(m - m_new)
            stats.append((m_new, alpha * l + jnp.sum(p, axis=0, keepdims=True), alpha))
            ps.append(p.astype(BF16))
        new = []
        for h in range(N_HEADS):
            vt = vt_ref[0, HEAD_DIM * h:HEAD_DIM * (h + 1), pl.ds(off, tk)]
            m_new, l_new, alpha = stats[h]
            acc_new = alpha * carry[3 * h + 2] + jnp.dot(vt, ps[h], preferred_element_type=F32)
            new += [m_new, l_new, acc_new]
        return tuple(new)

    init = (jnp.full((1, tq), NEG_BIG, F32), jnp.zeros((1, tq), F32),
            jnp.zeros((HEAD_DIM, tq), F32)) * N_HEADS
    res = lax.fori_loop(0, nk, chunk, init)
    outs = [res[3 * h + 2] / res[3 * h + 1] for h in range(N_HEADS)]
    o_t = jnp.concatenate(outs, axis=0).astype(BF16)
    eye = (lax.broadcasted_iota(I32, (tq, tq), 0) == lax.broadcasted_iota(I32, (tq, tq), 1))
    eye = jnp.where(eye, 1.0, 0.0).astype(BF16)
    o_ref[0] = lax.dot_general(eye, o_t, NT_DIMS, preferred_element_type=F32).astype(BF16)


def _attn_prompt_call(qi, wit, ki2, q, k, vt, *, tq):
    nb, seq, _ = q.shape
    return pl.pallas_call(
        functools.partial(_attn_prompt_kernel, tq=tq, tk=tq, seq=seq),
        grid=(nb, seq // tq),
        in_specs=[pl.BlockSpec((1, tq, ATTN_WIDTH), lambda b, i: (b, i, 0)),
                  pl.BlockSpec((1, IDX_HEADS, tq), lambda b, i: (b, 0, i)),
                  pl.BlockSpec((1, seq, LANES), lambda b, i: (b, 0, 0)),
                  pl.BlockSpec((1, tq, ATTN_WIDTH), lambda b, i: (b, i, 0)),
                  pl.BlockSpec((1, seq, ATTN_WIDTH), lambda b, i: (b, 0, 0)),
                  pl.BlockSpec((1, ATTN_WIDTH, seq), lambda b, i: (b, 0, 0))],
        out_specs=pl.BlockSpec((1, tq, ATTN_WIDTH), lambda b, i: (b, i, 0)),
        out_shape=jax.ShapeDtypeStruct((nb, seq, ATTN_WIDTH), BF16),
        scratch_shapes=[pltpu.VMEM((seq, tq), F32), pltpu.VMEM((seq, tq), F32)],
        compiler_params=_cparams(2),
        name="attn_prompt",
    )(qi, wit, ki2, q, k, vt)


def _chunk_rows(n_pages):
    return -(-(n_pages + 1) // 8) * 8


def _select_kernel(pt_ref, qi_ref, wi_ref, kinew_ref, kidx_hbm, bias_ref, kbuf, sem, sc_ref, *,
                   n_pages, nt, group):
    b = pl.program_id(0)
    nb = pl.num_programs(0)
    slot = b % 2
    n_rows_sc = _chunk_rows(n_pages)

    def page_copy(bb, sl, p):
        return pltpu.make_async_copy(kidx_hbm.at[0, pt_ref[bb, p]], kbuf.at[sl, p], sem.at[sl])

    def fetch(bb, sl):
        def body(p, carry):
            page_copy(bb, sl, p).start()
            return carry
        lax.fori_loop(0, n_pages, body, 0, unroll=8)

    @pl.when(b == 0)
    def _():
        fetch(0, 0)

    @pl.when(b + 1 < nb)
    def _():
        fetch(b + 1, 1 - slot)

    pltpu.make_async_copy(kidx_hbm.at[0, pl.ds(0, n_pages)], kbuf.at[slot], sem.at[slot]).wait()

    qi = qi_ref[0]
    wi = wi_ref[0]
    n_rows = IDX_HEADS * nt

    def head_sum(s):
        e = s[0:8]
        for r in range(1, n_rows // 8):
            e = e + s[8 * r:8 * r + 8]
        return e[0:nt] + e[nt:2 * nt]

    wi_g = jnp.concatenate([wi] * group, axis=1)

    def score_group(gi, carry):
        row0 = gi * group
        kc = jnp.concatenate([kbuf[slot, row0 + j] for j in range(group)], axis=1).astype(BF16)
        s = jnp.dot(qi, kc, preferred_element_type=F32)
        sc = head_sum(jnp.maximum(s, 0.0) * wi_g)
        for q in range(nt):
            for j in range(group):
                sc_ref[q, pl.ds(row0 + j, 1), :] = sc[q:q + 1, LANES * j:LANES * (j + 1)]
        return carry

    lax.fori_loop(0, n_pages // group, score_group, 0)

    s_new = lax.dot_general(qi, kinew_ref[0], NT_DIMS, preferred_element_type=F32)
    sc_new = head_sum(jnp.maximum(s_new, 0.0) * wi)
    lane = lax.broadcasted_iota(I32, (nt, LANES), 1)
    qrow = lax.broadcasted_iota(I32, (nt, LANES), 0)
    sc_new = jnp.where(lane <= qrow, sc_new, NEG_INF)
    for q in range(nt):
        sc_ref[q, n_pages:n_pages + 1, :] = sc_new[q:q + 1, :]
        sc_ref[q, n_pages + 1:n_rows_sc, :] = jnp.full((n_rows_sc - n_pages - 1, LANES), NEG_INF, F32)

    def reduce3(x, op):
        return op(op(x, axis=1, keepdims=True), axis=2, keepdims=True)

    def count_ge(trial, limit=None):
        sc = sc_ref[...]
        if limit is not None:
            sc = jnp.where(pos < limit, sc, NEG_INF)
        return reduce3(jnp.where(sc >= trial, 1.0, 0.0), jnp.sum)

    scores = sc_ref[...]
    pos = (lax.broadcasted_iota(I32, scores.shape, 1) * LANES + lax.broadcasted_iota(I32, scores.shape, 2))
    t, t_next = _exact_threshold(count_ge, (nt, 1, 1), two_bits=True)

    pos_bits = (n_rows_sc * LANES).bit_length()

    def tie_limit():
        need = float(TOPK) - count_ge(t_next)

        def bis(i, j):
            jt = j + lax.shift_left(jnp.int32(1), jnp.int32(pos_bits - 1) - i)
            return jnp.where(count_ge(t, jt) - count_ge(t_next, jt) <= need, jt, j)

        return lax.fori_loop(0, pos_bits, bis, jnp.zeros((nt, 1, 1), I32))

    jlim = lax.cond(jnp.max(count_ge(t)) > float(TOPK), tie_limit,
                    lambda: jnp.full((nt, 1, 1), 2 ** pos_bits - 1, I32))
    thr = jnp.where(pos < jlim, t, t_next)
    bias_ref[0] = jnp.where(scores >= thr, 0.0, NEG_BIG)


def _select_call(page_table, qi_s, wi_s, kinew, kidx_t, *, nt):
    nb, n_pages = page_table.shape
    rows = _chunk_rows(n_pages)
    grid_spec = pltpu.PrefetchScalarGridSpec(
        num_scalar_prefetch=1,
        grid=(nb,),
        in_specs=[pl.BlockSpec((1, IDX_HEADS * nt, IDX_DIM), lambda b, pt: (b, 0, 0)),
                  pl.BlockSpec((1, IDX_HEADS * nt, LANES), lambda b, pt: (b, 0, 0)),
                  pl.BlockSpec((1, LANES, IDX_DIM), lambda b, pt: (b, 0, 0)),
                  pl.BlockSpec(memory_space=pl.ANY)],
        out_specs=pl.BlockSpec((1, nt, rows, LANES), lambda b, pt: (b, 0, 0, 0)),
        scratch_shapes=[pltpu.VMEM((2, n_pages, IDX_DIM, PAGE_SIZE), F32),
                        pltpu.SemaphoreType.DMA((2,)),
                        pltpu.VMEM((nt, rows, LANES), F32)])
    return pl.pallas_call(
        functools.partial(_select_kernel, n_pages=n_pages, nt=nt, group=16),
        grid_spec=grid_spec,
        out_shape=jax.ShapeDtypeStruct((nb, nt, rows, LANES), F32),
        compiler_params=_cparams(1),
        name="select_sample",
    )(page_table, qi_s, wi_s, kinew, kidx_t)


Q_ROWS = 16
S_ROWS = 8


def _dense_sample_kernel(pt_ref, q_ref, bias_ref, biasn_ref, knew_ref, vnew_ref, ck_hbm, cv_hbm, o_ref,
                         kb, vb, sem, m_ref, l_ref, acc_ref, *, gp):
    b = pl.program_id(0)
    g = pl.program_id(1)
    ng = pl.num_programs(1)
    n_steps = pl.num_programs(0) * ng
    step = b * ng + g
    slot = step % 2

    def page_copies(st, sl, j):
        page = pt_ref[lax.div(st, ng), lax.rem(st, ng) * gp + j]
        return (pltpu.make_async_copy(ck_hbm.at[0, page], kb.at[sl, j], sem.at[0, sl]),
                pltpu.make_async_copy(cv_hbm.at[0, page], vb.at[sl, j], sem.at[1, sl]))

    def fetch(st, sl):
        for j in range(gp):
            for cp in page_copies(st, sl, j):
                cp.start()

    @pl.when(step == 0)
    def _():
        fetch(0, 0)

    @pl.when(step + 1 < n_steps)
    def _():
        fetch(step + 1, 1 - slot)

    for j in range(gp):
        for cp in page_copies(step, slot, j):
            cp.wait()

    @pl.when(g == 0)
    def _():
        m_ref[...] = jnp.full(m_ref.shape, NEG_BIG, F32)
        l_ref[...] = jnp.zeros(l_ref.shape, F32)
        acc_ref[...] = jnp.zeros(acc_ref.shape, F32)

    def attend(k_of, v_of, bias):
        ss = [jnp.dot(q_ref[0, h], k_of(h), preferred_element_type=F32)[0:S_ROWS] for h in range(N_HEADS)]
        ps, alphas = [], []
        for h in range(N_HEADS):
            s = ss[h] + bias
            m_old = m_ref[h]
            m_new = jnp.maximum(m_old, jnp.max(s, axis=1, keepdims=True))
            p = jnp.exp(s - m_new)
            alpha = jnp.exp(m_old - m_new)
            l_ref[h] = alpha * l_ref[h] + jnp.sum(p, axis=1, keepdims=True)
            m_ref[h] = m_new
            ps.append(jnp.concatenate([p, jnp.zeros((Q_ROWS - S_ROWS, p.shape[1]), F32)], axis=0).astype(BF16))
            alphas.append(alpha)
        for h in range(N_HEADS):
            pv = lax.dot_general(ps[h], v_of(h), NT_DIMS, preferred_element_type=F32)[0:S_ROWS]
            acc_ref[h] = alphas[h] * acc_ref[h] + pv

    def cat(buf, h):
        return jnp.concatenate([buf[slot, j, h] for j in range(gp)], axis=1).astype(BF16)

    attend(lambda h: cat(kb, h), lambda h: cat(vb, h),
           jnp.concatenate([bias_ref[0, j] for j in range(gp)], axis=1))

    @pl.when(g == ng - 1)
    def _():
        attend(lambda h: knew_ref[0, h].astype(BF16), lambda h: vnew_ref[0, h].astype(BF16), biasn_ref[0, 0])
        for h in range(N_HEADS):
            o_ref[0, h] = acc_ref[h] / l_ref[h]


def _dense_sample_call(page_table, qh, bias_t, knew_t, vnew_t, ck_t, cv_t, *, gp):
    nb, n_pages = page_table.shape
    grid_spec = pltpu.PrefetchScalarGridSpec(
        num_scalar_prefetch=1,
        grid=(nb, n_pages // gp),
        in_specs=[pl.BlockSpec((1, N_HEADS, Q_ROWS, HEAD_DIM), lambda b, g, pt: (b, 0, 0, 0)),
                  pl.BlockSpec((1, gp, S_ROWS, LANES), lambda b, g, pt: (b, g, 0, 0)),
                  pl.BlockSpec((1, 1, S_ROWS, LANES), lambda b, g, pt: (b, n_pages, 0, 0)),
                  pl.BlockSpec((1, N_HEADS, HEAD_DIM, LANES), lambda b, g, pt: (b, 0, 0, 0)),
                  pl.BlockSpec((1, N_HEADS, HEAD_DIM, LANES), lambda b, g, pt: (b, 0, 0, 0)),
                  pl.BlockSpec(memory_space=pl.ANY),
                  pl.BlockSpec(memory_space=pl.ANY)],
        out_specs=pl.BlockSpec((1, N_HEADS, S_ROWS, HEAD_DIM), lambda b, g, pt: (b, 0, 0, 0)),
        scratch_shapes=[pltpu.VMEM((2, gp, N_HEADS, HEAD_DIM, PAGE_SIZE), F32),
                        pltpu.VMEM((2, gp, N_HEADS, HEAD_DIM, PAGE_SIZE), F32),
                        pltpu.SemaphoreType.DMA((2, 2)),
                        pltpu.VMEM((N_HEADS, S_ROWS, 1), F32),
                        pltpu.VMEM((N_HEADS, S_ROWS, 1), F32),
                        pltpu.VMEM((N_HEADS, S_ROWS, HEAD_DIM), F32)])
    return pl.pallas_call(
        functools.partial(_dense_sample_kernel, gp=gp),
        grid_spec=grid_spec,
        out_shape=jax.ShapeDtypeStruct((nb, N_HEADS, S_ROWS, HEAD_DIM), F32),
        compiler_params=_cparams(2),
        name="dense_sample",
    )(page_table, qh, bias_t, bias_t, knew_t, vnew_t, ck_t, cv_t)


def _ffn_kernel(x_ref, at_ref, yb_ref, gm_ref, wga_ref, wgb_ref, wpa_ref, wpb_ref, wo_ref, gf_ref,
                wua_ref, wub_ref, fcw_ref, fcb_ref, wd_ref, gfin_ref, buf_ref,
                y_ref, st_ref, x1_ref, h2_ref, acc_ref, us_ref, carry_ref, *, tm, fc, tps, sample_nb):
    i = pl.program_id(0)
    c = pl.program_id(1)
    nc = pl.num_programs(1)

    @pl.when(c == 0)
    def _():
        x = x_ref[...]
        h = _rms(x, gm_ref[...]).astype(BF16)
        ga = jnp.dot(h, wga_ref[...], preferred_element_type=F32)
        gb = jnp.dot(h, wgb_ref[...], preferred_element_type=F32)
        ya = jnp.dot(at_ref[...], wpa_ref[...], preferred_element_type=F32)
        yb = jnp.dot(yb_ref[...], wpb_ref[...], preferred_element_type=F32)
        mix = _sigmoid(ga) * ya + _sigmoid(gb) * yb
        x1 = x + jnp.dot(mix.astype(BF16), wo_ref[...], preferred_element_type=F32)
        x1_ref[...] = x1
        h2_ref[...] = _rms(x1, gf_ref[...]).astype(BF16)
        acc_ref[...] = jnp.zeros(acc_ref.shape, F32)

    h2 = h2_ref[...]
    ua = jnp.dot(h2, wua_ref[...], preferred_element_type=F32)
    ub = jnp.dot(h2, wub_ref[...], preferred_element_type=F32)
    w = fcw_ref[...]
    if sample_nb:
        nb = sample_nb
        us = jnp.concatenate([buf_ref[...], ua], axis=0)
        uc = fcb_ref[...] + us[0:tm] * w[0:1]
        for j in range(1, FFN_CONV_W):
            uc = uc + us[j * nb:j * nb + tm] * w[j:j + 1]
        st_ref[...] = us[tm:tm + (FFN_CONV_W - 1) * nb]
    else:
        first = (i % tps) == 0
        us_ref[0:8, :] = jnp.where(first, jnp.zeros((8, fc), F32), carry_ref[c])
        us_ref[8:8 + tm, :] = ua
        uc = fcb_ref[...] + us_ref[6:6 + tm, :] * w[0:1]
        uc = uc + us_ref[7:7 + tm, :] * w[1:2]
        uc = uc + ua * w[2:3]
        tail = ua[tm - 8:tm, :]
        carry_ref[c] = tail
        st_ref[0] = tail
    act = (_gelu(uc) * ub).astype(BF16)
    acc_ref[...] += jnp.dot(act, wd_ref[...], preferred_element_type=F32)

    @pl.when(c == nc - 1)
    def _():
        y_ref[...] = _rms(x1_ref[...] + acc_ref[...], gfin_ref[...])


def _ffn_call(x2d, attn, yb, gm, wga, wgb, wpa, wpb, wo, gf, wup, fcw, fcb, wd, gfin, buf, *,
              tm, fc, seq, sample_nb):
    n, d = x2d.shape
    d_ff = wd.shape[0]
    nc = d_ff // fc
    tps = seq // tm if not sample_nb else 1
    tok = lambda w: pl.BlockSpec((tm, w), lambda i, c: (i, 0))
    const = lambda a: pl.BlockSpec(a.shape, lambda i, c: (0,) * a.ndim, pipeline_mode=pl.Buffered(1))
    wmode = pl.Buffered(1) if nc == 1 else None
    if sample_nb:
        nst = (FFN_CONV_W - 1) * sample_nb
        buf_spec = pl.BlockSpec((nst, fc), lambda i, c: (0, c))
        st_spec = pl.BlockSpec((nst, fc), lambda i, c: (0, c))
        st_shape = jax.ShapeDtypeStruct((nst, d_ff), F32)
    else:
        buf_spec = pl.BlockSpec((8, LANES), lambda i, c: (0, 0))
        st_spec = pl.BlockSpec((1, 8, fc), lambda i, c: (i, 0, c))
        st_shape = jax.ShapeDtypeStruct((n // tm, 8, d_ff), F32)
    return pl.pallas_call(
        functools.partial(_ffn_kernel, tm=tm, fc=fc, tps=tps, sample_nb=sample_nb),
        grid=(n // tm, nc),
        in_specs=[tok(d), tok(ATTN_WIDTH), tok(LRU_WIDTH), const(gm), const(wga), const(wgb), const(wpa),
                  const(wpb), const(wo), const(gf),
                  pl.BlockSpec((d, fc), lambda i, c: (0, c), pipeline_mode=wmode),
                  pl.BlockSpec((d, fc), lambda i, c: (0, nc + c), pipeline_mode=wmode),
                  pl.BlockSpec((FFN_CONV_W, fc), lambda i, c: (0, c)),
                  pl.BlockSpec((1, fc), lambda i, c: (0, c)),
                  pl.BlockSpec((fc, d), lambda i, c: (c, 0), pipeline_mode=wmode),
                  const(gfin), buf_spec],
        out_specs=(tok(d), st_spec),
        out_shape=(jax.ShapeDtypeStruct((n, d), F32), st_shape),
        scratch_shapes=[pltpu.VMEM((tm, d), F32), pltpu.VMEM((tm, d), BF16), pltpu.VMEM((tm, d), F32),
                        pltpu.VMEM((tm + 8, fc), F32), pltpu.VMEM((nc, 8, fc), F32)],
        compiler_params=_cparams(2),
        name="ffn_sample" if sample_nb else "ffn_prompt",
    )(x2d, attn, yb, gm, wga, wgb, wpa, wpb, wo, gf, wup, wup, fcw, fcb, wd, gfin, buf)


def _rope_tables(pos):
    half = HEAD_DIM // 2
    inv = jnp.power(ROPE_THETA, -jnp.arange(half, dtype=F32) / half)
    ang = pos.astype(F32)[:, None] * inv[None, :]
    cos, sin = jnp.cos(ang), jnp.sin(ang)
    z = jnp.zeros_like(sin)
    tile = lambda a, b: jnp.concatenate([a, b, a, b], axis=1)
    return tile(cos, cos), tile(-sin, z), tile(z, sin), cos.T, sin.T


def _block_diag(w):
    nblk, bw, _ = w.shape
    eye = jnp.eye(nblk, dtype=w.dtype)
    return (eye[:, None, :, None] * w[:, :, None, :]).reshape(nblk * bw, nblk * bw)


def kernel(x_prompt, x_sample, cache_k, cache_v, cache_kidx, page_table, state_lru_conv, state_lru_h,
           state_ffn_conv, norm_mix_g, w_in, lru_conv_w, lru_conv_b, lru_wa, lru_ba, lru_wx, lru_bx,
           lru_lambda, w_proj_a, w_proj_b, w_out, norm_ffn_g, w_up, ffn_conv_w, ffn_conv_b, w_down,
           norm_final_g):
    depth = w_in.shape[0]
    assert depth == 1, "single-layer step"
    nbp, seq, d = x_prompt.shape
    nbs, nts, _ = x_sample.shape
    n_pages = page_table.shape[1]
    past = n_pages * PAGE_SIZE
    d_ff = w_down.shape[1]
    a = ATTN_WIDTH
    row = lambda v: v.reshape(1, -1)

    win = w_in[0]
    o_ki = 4 * a
    o_wi = o_ki + IDX_DIM
    o_xl = o_wi + IDX_HEADS
    o_gl = o_xl + LRU_WIDTH
    o_ga = o_gl + LRU_WIDTH
    o_gb = o_ga + d
    wbig = win[:, :4 * a].astype(BF16)
    wbig_p = jnp.concatenate([wbig[:, :2 * a], wbig[:, 3 * a:]], axis=1)
    wsm = jnp.pad(win[:, o_ki:o_xl], ((0, 0), (0, LANES - IDX_DIM - IDX_HEADS))).astype(BF16)
    wkvt = win[:, a:3 * a].T.astype(BF16)
    wxl = win[:, o_xl:o_gl].astype(BF16)
    wgl = win[:, o_gl:o_ga].astype(BF16)
    wga = win[:, o_ga:o_gb].astype(BF16)
    wgb = win[:, o_gb:o_gb + d].astype(BF16)
    wa_bd = _block_diag(lru_wa[0]).astype(BF16)
    wx_bd = _block_diag(lru_wx[0]).astype(BF16)
    wpa = w_proj_a[0].astype(BF16)
    wpb = w_proj_b[0].astype(BF16)
    wo = w_out[0].astype(BF16)
    wup = w_up[0].astype(BF16)
    wd = w_down[0].astype(BF16)
    gm, gf, gfin = row(norm_mix_g[0]), row(norm_ffn_g[0]), row(norm_final_g)
    lru_args = (lru_conv_w[0], row(lru_conv_b[0]), wa_bd, row(lru_ba[0]), wx_bd, row(lru_bx[0]),
                row(lru_lambda[0]))
    ffn_w = (gm, wga, wgb, wpa, wpb, wo, gf, wup, ffn_conv_w[0], row(ffn_conv_b[0]), wd, gfin)

    xp2 = x_prompt.reshape(nbp * seq, d)
    q_p, qi_p, kb_p, ktf_p, vtf_p, vt_p, kiwi_p, ki2_p = _qkv_call(
        xp2, gm, wbig_p, wsm, wkvt, *_rope_tables(jnp.arange(seq, dtype=I32)), tm=1024, seq=seq, prompt=True)
    wit_p = jnp.swapaxes(kiwi_p[:, IDX_DIM:IDX_DIM + IDX_HEADS].reshape(nbp, seq, IDX_HEADS), 1, 2)
    sh3 = lambda t, w: t.reshape(nbp, seq, w)
    attn_p = _attn_prompt_call(sh3(qi_p, a), wit_p, sh3(ki2_p, LANES), sh3(q_p, a), sh3(kb_p, a), vt_p, tq=256)
    yb_p, lconv_p, lh_p = _lru_prompt_call(x_prompt, gm, wxl, wgl, *lru_args, tc=1024)
    y_p, fconv_p = _ffn_call(xp2, attn_p.reshape(nbp * seq, a), yb_p.reshape(nbp * seq, LRU_WIDTH), *ffn_w,
                             jnp.zeros((8, LANES), F32), tm=512, fc=d_ff, seq=seq, sample_nb=0)

    ns = nbs * nts
    xs2 = jnp.swapaxes(x_sample, 0, 1).reshape(ns, d)
    pos_s = jnp.repeat(past + jnp.arange(nts, dtype=I32), nbs)
    q_s, qi_s, kf_s, vf_s, kiwi_s = _qkv_call(xs2, gm, wbig, wsm, wkvt, *_rope_tables(pos_s),
                                              tm=ns, seq=nts, prompt=False)
    bm = lambda t, w: jnp.swapaxes(t.reshape(nts, nbs, w), 0, 1)
    k_s4 = bm(kf_s, a).reshape(nbs, nts, N_HEADS, HEAD_DIM)
    v_s4 = bm(vf_s, a).reshape(nbs, nts, N_HEADS, HEAD_DIM)
    ki_s = bm(kiwi_s[:, :IDX_DIM], IDX_DIM)
    wi_s = bm(kiwi_s[:, IDX_DIM:IDX_DIM + IDX_HEADS], IDX_HEADS)
    qi_hq = jnp.swapaxes(bm(qi_s, a).reshape(nbs, nts, IDX_HEADS, IDX_DIM), 1, 2).reshape(
        nbs, IDX_HEADS * nts, IDX_DIM)
    wi_hq = jnp.broadcast_to(jnp.swapaxes(wi_s, 1, 2).reshape(nbs, IDX_HEADS * nts, 1),
                             (nbs, IDX_HEADS * nts, LANES))
    kinew = jnp.pad(ki_s.astype(BF16), ((0, 0), (0, LANES - nts), (0, 0)))
    kidx_t = jnp.transpose(cache_kidx, (0, 1, 3, 2))
    ck_t = jnp.transpose(cache_k, (0, 1, 3, 4, 2))
    cv_t = jnp.transpose(cache_v, (0, 1, 3, 4, 2))
    bias_s = _select_call(page_table, qi_hq, wi_hq, kinew, kidx_t, nt=nts)
    bias_t = jnp.pad(jnp.swapaxes(bias_s, 1, 2), ((0, 0), (0, 0), (0, S_ROWS - nts), (0, 0)))
    qh = jnp.swapaxes(bm(q_s, a).reshape(nbs, nts, N_HEADS, HEAD_DIM), 1, 2)
    qh = jnp.pad(qh, ((0, 0), (0, 0), (0, Q_ROWS - nts), (0, 0))).astype(BF16)
    new_t = lambda t: jnp.pad(jnp.transpose(t, (0, 2, 3, 1)), ((0, 0), (0, 0), (0, 0), (0, LANES - nts)))
    o_s = _dense_sample_call(page_table, qh, bias_t, new_t(k_s4), new_t(v_s4), ck_t, cv_t, gp=32)
    attn_s = jnp.transpose(o_s[:, :, :nts], (2, 0, 1, 3)).reshape(ns, a).astype(BF16)
    tmaj = lambda s: jnp.swapaxes(s, 0, 1).reshape(-1, s.shape[-1])
    yb_s, lconv_s, lh_s = _lru_sample_call(xs2, gm, wxl, wgl, *lru_args, tmaj(state_lru_conv[0]),
                                           state_lru_h[0], nb=nbs, nt=nts)
    y_s, fconv_s = _ffn_call(xs2, attn_s, yb_s, *ffn_w, tmaj(state_ffn_conv[0]),
                             tm=ns, fc=1024, seq=nts, sample_nb=nbs)
    bmaj = lambda t, r: jnp.swapaxes(t.reshape(r, nbs, t.shape[-1]), 0, 1)

    return (y_p.reshape(nbp, seq, d),
            bmaj(y_s, nts),
            jnp.transpose(ktf_p.reshape(nbp, N_HEADS, HEAD_DIM, seq), (0, 3, 1, 2))[None],
            jnp.transpose(vtf_p.reshape(nbp, N_HEADS, HEAD_DIM, seq), (0, 3, 1, 2))[None],
            kiwi_p[:, :IDX_DIM].reshape(1, nbp, seq, IDX_DIM),
            lconv_p[None],
            lh_p.reshape(1, nbp, LRU_WIDTH),
            fconv_p.reshape(nbp, -1, 8, d_ff)[:, -1, 8 - (FFN_CONV_W - 1):][None],
            k_s4[None],
            v_s4[None],
            ki_s[None],
            bmaj(lconv_s, LRU_CONV_W - 1)[None],
            lh_s[None],
            bmaj(fconv_s, FFN_CONV_W - 1)[None])
```
